```python
import math
import jax
import jax.numpy as jnp
from jax import lax
import numpy as np

D_MODEL = 1024
BATCH = 1
SEQ = 16384
DEPTH = 4

D_MIX = D_MODEL
GROUP_W = D_MIX // 4
HEAD_DIM = 64
CHUNK = 128
EPS = 1e-6

RET_HEADS = GROUP_W // HEAD_DIM
ROPE_BASE = 10000.0

SSM_HEADS = GROUP_W // HEAD_DIM
SSM_GROUPS = 2
SSM_STATE = 128
SSM_CONV = 5

HY_CONV = 3
HY_EMB = 33
HY_BANDS = (HY_EMB - 1) // 2
HY_ORDER = 64
HY_INNER = 2
HY_DECAY_TARGET = 1e-2
HY_FAST_PCT = 0.3
HY_SLOW_PCT = 1.5
HY_SHIFT = 0.05

GDN_HEADS = GROUP_W // HEAD_DIM
GDN_CONV = 5

N_EXPERTS = 16
CAPACITY_FACTOR = 2
EXPERT_FF = 1024

PLE_DIM = 256

RET_COLS = 4 * GROUP_W
SSM_XBC = GROUP_W + 2 * SSM_GROUPS * SSM_STATE
SSM_COLS = GROUP_W + SSM_XBC + 2 * SSM_HEADS
HY_COLS = 3 * GROUP_W
GDN_COLS = 4 * GROUP_W + 4 * GDN_HEADS
N_IN = RET_COLS + SSM_COLS + HY_COLS + GDN_COLS

kernel_name = 'hybrid_parallel_heads_encoder'


def _split(x, sizes):
    offs = np.cumsum(sizes)[:-1].tolist()
    return jnp.split(x, offs, axis=-1)


def _flip(a):
    return a[:, ::-1]


def rmsnorm(x, w):
    xf = x.astype(jnp.float32)
    y = xf * lax.rsqrt(jnp.mean(xf * xf, axis=-1, keepdims=True) + EPS)
    return (y * w.astype(jnp.float32)).astype(x.dtype)


def l2norm(x):
    return x * lax.rsqrt(jnp.sum(x * x, axis=-1, keepdims=True) + EPS)


def dwconv_centred(x, w, bias=None):
    k, c = w.shape
    y = lax.conv_general_dilated(x, w[:, None, :].astype(x.dtype), (1,), [((k - 1) // 2, k // 2)],
                                 dimension_numbers=('NWC', 'WIO', 'NWC'), feature_group_count=c)
    return y if bias is None else y + bias


def rotary(x, pos):
    d = x.shape[-1]
    inv = ROPE_BASE ** (-jnp.arange(0, d, 2, dtype=jnp.float32) / d)
    ang = pos[:, None] * inv[None]
    cos, sin = jnp.cos(ang)[None, :, None, :], jnp.sin(ang)[None, :, None, :]
    x1, x2 = x[..., : d // 2], x[..., d // 2:]
    return jnp.concatenate([x1 * cos - x2 * sin, x1 * sin + x2 * cos], axis=-1)


def retention_mixer(cols, gn_w):
    nb, L, _ = cols.shape
    H, d, T, nc = RET_HEADS, HEAD_DIM, CHUNK, L // CHUNK
    q, k, v, g = _split(cols, [GROUP_W] * 4)
    pos = jnp.arange(L, dtype=jnp.float32)
    q = rotary(q.reshape(nb, L, H, d), pos)
    k = rotary(k.reshape(nb, L, H, d), pos) * d ** -0.5
    v = v.reshape(nb, L, H, d)
    lg = jnp.log(1.0 - 2.0 ** (-5.0 - jnp.arange(H, dtype=jnp.float32)))
    t = jnp.arange(T, dtype=jnp.float32)
    dmask = jnp.exp(jnp.abs(t[:, None] - t[None, :])[None] * lg[:, None, None])
    qc, kc, vc = [a.reshape(nb, nc, T, H, d) for a in (q, k, v)]
    s = jnp.einsum('bcihd,bcjhd->bchij', qc, kc) * dmask
    o = jnp.einsum('bchij,bcjhe->bcihe', s, vc)
    pw_tail = jnp.exp((T - t)[:, None] * lg)
    pw_head = jnp.exp(t[:, None] * lg)
    u_fwd = jnp.einsum('bcjhd,bcjhe->bchde', kc * pw_tail[:, :, None], vc)
    u_bwd = jnp.einsum('bcjhd,bcjhe->bchde', kc * pw_head[:, :, None], vc)
    dec = jnp.exp(T * lg)[:, None, None]

    def step(S, u):
        return dec * S + u, S

    s0 = jnp.zeros((nb, H, d, d), jnp.float32)
    _, prev = lax.scan(step, s0, jnp.moveaxis(u_fwd, 1, 0))
    _, nxt = lax.scan(step, s0, jnp.moveaxis(u_bwd, 1, 0), reverse=True)
    prev, nxt = jnp.moveaxis(prev, 0, 1), jnp.moveaxis(nxt, 0, 1)
    o = (o + jnp.einsum('bcihd,bchde->bcihe', qc * pw_head[:, :, None], prev)
           + jnp.einsum('bcihd,bchde->bcihe', qc * pw_tail[:, :, None], nxt))
    o = o.reshape(nb, L, H, d)
    mu = jnp.mean(o, axis=-1, keepdims=True)
    var = jnp.mean(jnp.square(o - mu), axis=-1, keepdims=True)
    o = ((o - mu) * lax.rsqrt(var + EPS)).reshape(nb, L, GROUP_W) * gn_w
    return jax.nn.silu(g) * o


def ssd_chunked(x, dt, A, Bm, Cm):
    nb, L, H, P = x.shape
    N = Bm.shape[-1]
    T, nc = CHUNK, L // CHUNK
    xc = (x * dt[..., None]).reshape(nb, nc, T, H, P)
    Bc = Bm.reshape(nb, nc, T, H, N)
    Cc = Cm.reshape(nb, nc, T, H, N)
    a_cum = jnp.cumsum((dt * A).reshape(nb, nc, T, H), axis=2)
    tri = jnp.tril(jnp.ones((T, T), bool))[None, None, :, :, None]
    seg = a_cum[:, :, :, None, :] - a_cum[:, :, None, :, :]
    lmask = jnp.exp(jnp.where(tri, seg, -jnp.inf))
    scores = jnp.einsum('bcihn,bcjhn->bcijh', Cc, Bc) * lmask
    y = jnp.einsum('bcijh,bcjhp->bcihp', scores, xc)
    decay_to_end = jnp.exp(a_cum[:, :, -1:, :] - a_cum)
    states = jnp.einsum('bcjhn,bcjhp->bchpn', Bc * decay_to_end[..., None], xc)
    chunk_decay = jnp.exp(a_cum[:, :, -1, :])

    def step(S, inp):
        st, dc = inp
        return dc[:, :, None, None] * S + st, S

    _, prev = lax.scan(step, jnp.zeros((nb, H, P, N), jnp.float32),
                       (jnp.moveaxis(states, 1, 0), jnp.moveaxis(chunk_decay, 1, 0)))
    prev = jnp.moveaxis(prev, 0, 1)
    y = y + jnp.einsum('bcihn,bchpn->bcihp', Cc * jnp.exp(a_cum)[..., None], prev)
    return y.reshape(nb, L, H, P)


def mamba2_mixer(cols, conv_w, conv_b, a_log, dt_bias, d_skip, norm_w):
    nb, L, _ = cols.shape
    z, xbc, dt_raw = _split(cols, [GROUP_W, SSM_XBC, 2 * SSM_HEADS])
    xbc = jax.nn.silu(dwconv_centred(xbc, conv_w, conv_b))
    xs, Bm, Cm = _split(xbc, [GROUP_W, SSM_GROUPS * SSM_STATE, SSM_GROUPS * SSM_STATE])
    rep = SSM_HEADS // SSM_GROUPS
    xs = xs.reshape(nb, L, SSM_HEADS, HEAD_DIM)
    Bm = jnp.repeat(Bm.reshape(nb, L, SSM_GROUPS, SSM_STATE), rep, axis=2)
    Cm = jnp.repeat(Cm.reshape(nb, L, SSM_GROUPS, SSM_STATE), rep, axis=2)
    dt_raw = dt_raw.reshape(nb, L, 2, SSM_HEADS)
    dt_f = jax.nn.softplus(dt_raw[:, :, 0] + dt_bias[0])
    dt_b = jax.nn.softplus(dt_raw[:, :, 1] + dt_bias[1])
    y_f = ssd_chunked(xs, dt_f, -jnp.exp(a_log[0]), Bm, Cm)
    y_b = _flip(ssd_chunked(_flip(xs), _flip(dt_b), -jnp.exp(a_log[1]), _flip(Bm), _flip(Cm)))
    y = (y_f + y_b + xs * d_skip[:, None]).reshape(nb, L, GROUP_W)
    return rmsnorm(y * jax.nn.silu(z), norm_w)


def hyena_filters(L, w_in, b_in, w_mid, b_mid, freq, w_out):
    t = jnp.linspace(0.0, 1.0, L, dtype=jnp.float32)[:, None]
    bands = jnp.linspace(1e-4, HY_BANDS - 1, HY_BANDS, dtype=jnp.float32)
    ang = (2.0 * math.pi / L) * jnp.arange(L, dtype=jnp.float32)[:, None] * bands[None]
    z = jnp.concatenate([t, jnp.cos(ang), -jnp.sin(ang)], axis=-1)
    h = jnp.sin(freq * (z @ w_in + b_in))
    for j in range(HY_INNER):
        h = jnp.sin(freq * (h @ w_mid[j] + b_mid[j]))
    h = h @ w_out
    max_decay = math.log(HY_DECAY_TARGET) / HY_FAST_PCT
    min_decay = math.log(HY_DECAY_TARGET) / HY_SLOW_PCT
    deltas = jnp.tile(jnp.linspace(min_decay, max_decay, GROUP_W, dtype=jnp.float32), 2)
    h = h * (jnp.exp(-t * jnp.abs(deltas)) + HY_SHIFT)
    h_fwd, h_bwd = h[:, :GROUP_W], h[:, GROUP_W:]
    return jnp.concatenate([h_fwd, jnp.zeros((1, GROUP_W), jnp.float32), h_bwd[1:][::-1]], axis=0)


def hyena_mixer(cols, conv_w, conv_b, w_in, b_in, w_mid, b_mid, freq, w_out, bias, norm_w):
    L = cols.shape[1]
    u = dwconv_centred(cols, conv_w, conv_b)
    x0, x1, v = _split(u, [GROUP_W] * 3)
    v = v * x1
    filt = hyena_filters(L, w_in, b_in, w_mid, b_mid, freq, w_out)
    v_f = jnp.fft.rfft(v, n=2 * L, axis=1)
    f_f = jnp.fft.rfft(filt, n=2 * L, axis=0)
    y = jnp.fft.irfft(v_f * f_f[None], n=2 * L, axis=1)[:, :L] + v * bias
    return rmsnorm(y * x0, norm_w)


def gated_delta_chunked(q, k, v, g, beta):
    nb, L, H, dk = q.shape
    dv = v.shape[-1]
    T, nc = CHUNK, L // CHUNK

    def chunks(a):
        return jnp.moveaxis(a.reshape(nb, nc, T, H, -1), 3, 2)

    qc, kc, vc = chunks(q), chunks(k), chunks(v)
    gc = jnp.cumsum(chunks(g[..., None])[..., 0], axis=-1)
    bc = chunks(beta[..., None])
    incl = jnp.tril(jnp.ones((T, T), bool))
    strict = jnp.tril(jnp.ones((T, T), bool), -1)
    decay = jnp.exp(jnp.where(incl, gc[..., :, None] - gc[..., None, :], -jnp.inf))
    a_strict = jnp.where(strict, jnp.einsum('bchid,bchjd->bchij', kc * bc, kc) * decay, 0.0)
    eye = jnp.broadcast_to(jnp.eye(T, dtype=jnp.float32), a_strict.shape)
    tinv = lax.linalg.triangular_solve(a_strict, eye, left_side=True, lower=True, unit_diagonal=True)
    u = tinv @ (vc * bc)
    w = tinv @ (kc * bc * jnp.exp(gc)[..., None])
    qk = jnp.where(incl, jnp.einsum('bchid,bchjd->bchij', qc, kc) * decay, 0.0)
    q_dec = qc * jnp.exp(gc)[..., None]
    g_last = gc[..., -1]
    k_dec = kc * jnp.exp(g_last[..., None] - gc)[..., None]

    def step(S, inp):
        q_i, k_i, u_i, w_i, qk_i, gl_i = inp
        v_new = u_i - jnp.einsum('bhtk,bhkv->bhtv', w_i, S)
        o_i = jnp.einsum('bhtk,bhkv->bhtv', q_i, S) + jnp.einsum('bhts,bhsv->bhtv', qk_i, v_new)
        S = S * jnp.exp(gl_i)[..., None, None] + jnp.einsum('bhtk,bhtv->bhkv', k_i, v_new)
        return S, o_i

    xs = tuple(jnp.moveaxis(a, 1, 0) for a in (q_dec, k_dec, u, w, qk, g_last))
    _, o = lax.scan(step, jnp.zeros((nb, H, dk, dv), jnp.float32), xs)
    o = jnp.moveaxis(jnp.moveaxis(o, 0, 1), 2, 3)
    return o.reshape(nb, L, H, dv)


def gdn_mixer(cols, conv_w, a_log, dt_bias, norm_w):
    nb, L, _ = cols.shape
    qkv, z, a_raw, b_raw = _split(cols, [3 * GROUP_W, GROUP_W, 2 * GDN_HEADS, 2 * GDN_HEADS])
    qkv = jax.nn.silu(dwconv_centred(qkv, conv_w))
    q, k, v = [a.reshape(nb, L, GDN_HEADS, HEAD_DIM) for a in _split(qkv, [GROUP_W] * 3)]
    q = l2norm(q) * HEAD_DIM ** -0.5
    k = l2norm(k)
    a_raw = a_raw.reshape(nb, L, 2, GDN_HEADS)
    b_raw = b_raw.reshape(nb, L, 2, GDN_HEADS)
    g_f = -jnp.exp(a_log[0]) * jax.nn.softplus(a_raw[:, :, 0] + dt_bias[0])
    g_b = -jnp.exp(a_log[1]) * jax.nn.softplus(a_raw[:, :, 1] + dt_bias[1])
    beta_f = jax.nn.sigmoid(b_raw[:, :, 0])
    beta_b = jax.nn.sigmoid(b_raw[:, :, 1])
    o = (gated_delta_chunked(q, k, v, g_f, beta_f)
         + _flip(gated_delta_chunked(_flip(q), _flip(k), _flip(v), _flip(g_b), _flip(beta_b))))
    o = rmsnorm(o, norm_w).reshape(nb, L, GROUP_W)
    return o * jax.nn.silu(z)


def expert_choice_moe(x, router_w, w_gate, w_up, w_down):
    nb, L, D = x.shape
    cap = CAPACITY_FACTOR * L // N_EXPERTS
    aff = jax.nn.softmax(jnp.einsum('bld,de->ble', x, router_w).astype(jnp.float32), axis=-1)
    gate, idx = lax.top_k(jnp.swapaxes(aff, 1, 2), cap)
    xs = jax.vmap(lambda xb, ib: xb[ib])(x, idx)
    h = jax.nn.silu(jnp.einsum('becd,edf->becf', xs, w_gate)) * jnp.einsum('becd,edf->becf', xs, w_up)
    ye = jnp.einsum('becf,efd->becd', h, w_down) * gate[..., None].astype(x.dtype)
    return jax.vmap(lambda ib, yb: jnp.zeros((L, D), yb.dtype).at[ib.reshape(-1)].add(yb.reshape(-1, D)))(idx, ye)


def setup_inputs(seed: int = 0) -> dict:
    key = jax.random.key(seed)
    ks = iter(jax.random.split(key, 40))

    def nrm(shape, scale):
        return jax.random.normal(next(ks), shape, jnp.float32) * scale

    def gain(shape):
        return 1.0 + nrm(shape, 0.02)

    def a_log(shape):
        return jnp.log(jax.random.uniform(next(ks), shape, jnp.float32, 1.0, 16.0))

    def dt_bias(shape):
        dt = jnp.exp(jax.random.uniform(next(ks), shape, jnp.float32, math.log(1e-3), math.log(1e-1)))
        return dt + jnp.log(-jnp.expm1(-dt))

    n = DEPTH
    return {
        'x': nrm((BATCH, SEQ, D_MODEL), 1.0),
        'p': nrm((DEPTH, BATCH, SEQ, PLE_DIM), 1.0),
        'norm_mix_w': gain((n, D_MODEL)),
        'w_in': nrm((n, D_MODEL, N_IN), D_MODEL ** -0.5),
        'ret_gn_w': gain((n, GROUP_W)),
        'ssm_conv_w': nrm((n, SSM_CONV, SSM_XBC), SSM_CONV ** -0.5),
        'ssm_conv_b': nrm((n, SSM_XBC), 0.01),
        'ssm_a_log': a_log((n, 2, SSM_HEADS)),
        'ssm_dt_bias': dt_bias((n, 2, SSM_HEADS)),
        'ssm_d': gain((n, SSM_HEADS)),
        'ssm_norm_w': gain((n, GROUP_W)),
        'hy_conv_w': nrm((n, HY_CONV, HY_COLS), HY_CONV ** -0.5),
        'hy_conv_b': nrm((n, HY_COLS), 0.01),
        'hy_filt_w_in': nrm((n, HY_EMB, HY_ORDER), HY_EMB ** -0.5),
        'hy_filt_b_in': nrm((n, HY_ORDER), 0.01),
        'hy_filt_w_mid': nrm((n, HY_INNER, HY_ORDER, HY_ORDER), HY_ORDER ** -0.5),
        'hy_filt_b_mid': nrm((n, HY_INNER, HY_ORDER), 0.01),
        'hy_filt_freq': gain((n, HY_ORDER)),
        'hy_filt_w_out': nrm((n, HY_ORDER, 2 * GROUP_W), HY_ORDER ** -0.5),
        'hy_bias': nrm((n, GROUP_W), 0.1),
        'hy_norm_w': gain((n, GROUP_W)),
        'gdn_conv_w': nrm((n, GDN_CONV, 3 * GROUP_W), GDN_CONV ** -0.5),
        'gdn_a_log': a_log((n, 2, GDN_HEADS)),
        'gdn_dt_bias': dt_bias((n, 2, GDN_HEADS)),
        'gdn_norm_w': gain((n, HEAD_DIM)),
        'w_out': nrm((n, D_MIX, D_MODEL), D_MIX ** -0.5),
        'norm_ffn_w': gain((n, D_MODEL)),
        'router_w': nrm((n, D_MODEL, N_EXPERTS), D_MODEL ** -0.5),
        'exp_w_gate': nrm((n, N_EXPERTS, D_MODEL, EXPERT_FF), D_MODEL ** -0.5),
        'exp_w_up': nrm((n, N_EXPERTS, D_MODEL, EXPERT_FF), D_MODEL ** -0.5),
        'exp_w_down': nrm((n, N_EXPERTS, EXPERT_FF, D_MODEL), EXPERT_FF ** -0.5),
        'ple_norm_w': gain((n, D_MODEL)),
        'ple_gate_w': nrm((n, D_MODEL, D_MODEL), D_MODEL ** -0.5),
        'ple_proj_w': nrm((n, PLE_DIM, D_MODEL), PLE_DIM ** -0.5),
        'final_norm_w': gain((D_MODEL,)),
    }


def reference(x, p, norm_mix_w, w_in, ret_gn_w, ssm_conv_w, ssm_conv_b, ssm_a_log, ssm_dt_bias,
              ssm_d, ssm_norm_w, hy_conv_w, hy_conv_b, hy_filt_w_in, hy_filt_b_in, hy_filt_w_mid,
              hy_filt_b_mid, hy_filt_freq, hy_filt_w_out, hy_bias, hy_norm_w, gdn_conv_w, gdn_a_log,
              gdn_dt_bias, gdn_norm_w, w_out, norm_ffn_w, router_w, exp_w_gate, exp_w_up, exp_w_down,
              ple_norm_w, ple_gate_w, ple_proj_w, final_norm_w):
    h = x
    for i in range(DEPTH):
        hn = rmsnorm(h, norm_mix_w[i])
        cols = jnp.einsum('bld,dn->bln', hn, w_in[i]).astype(jnp.float32)
        c_ret, c_ssm, c_hy, c_gdn = _split(cols, [RET_COLS, SSM_COLS, HY_COLS, GDN_COLS])
        mixed = jnp.concatenate([
            retention_mixer(c_ret, ret_gn_w[i]),
            mamba2_mixer(c_ssm, ssm_conv_w[i], ssm_conv_b[i], ssm_a_log[i], ssm_dt_bias[i],
                         ssm_d[i], ssm_norm_w[i]),
            hyena_mixer(c_hy, hy_conv_w[i], hy_conv_b[i], hy_filt_w_in[i], hy_filt_b_in[i],
                        hy_filt_w_mid[i], hy_filt_b_mid[i], hy_filt_freq[i], hy_filt_w_out[i],
                        hy_bias[i], hy_norm_w[i]),
            gdn_mixer(c_gdn, gdn_conv_w[i], gdn_a_log[i], gdn_dt_bias[i], gdn_norm_w[i]),
        ], axis=-1).astype(h.dtype)
        h = h + jnp.einsum('bln,nd->bld', mixed, w_out[i])
        h = h + expert_choice_moe(rmsnorm(h, norm_ffn_w[i]), router_w[i], exp_w_gate[i],
                                  exp_w_up[i], exp_w_down[i])
        gate = jax.nn.sigmoid(jnp.einsum('bld,de->ble', rmsnorm(h, ple_norm_w[i]), ple_gate_w[i]))
        h = h + gate * jnp.einsum('blk,kd->bld', p[i], ple_proj_w[i])
    return rmsnorm(h, final_norm_w)
```

```python
import functools
import math

import numpy as np
import jax
import jax.numpy as jnp
from jax import lax
from jax.experimental import pallas as pl
from jax.experimental.pallas import tpu as pltpu

F32 = jnp.float32
BF16 = jnp.bfloat16
I32 = jnp.int32
HIGHEST = lax.Precision.HIGHEST

D_MODEL = 1024
GROUP_W = 256
HEAD_DIM = 64
N_HEADS = GROUP_W // HEAD_DIM
CHUNK = 128
EPS = 1e-6
ROPE_BASE = 10000.0
SSM_STATE = 128
SSM_CONV = 5
XBC_W = 3 * GROUP_W
HY_CONV = 3
HY_EMB = 33
HY_EMB_PAD = 64
HY_BANDS = (HY_EMB - 1) // 2
HY_ORDER = 64
HY_INNER = 2
HY_DECAY_TARGET = 1e-2
HY_FAST_PCT = 0.3
HY_SLOW_PCT = 1.5
HY_SHIFT = 0.05
GDN_CONV = 5
N_EXPERTS = 16
CAPACITY_FACTOR = 2
EXPERT_FF = 1024
PLE_DIM = 256

HALO = 8
FFT_N2 = 256
ROW_TILE = 128
TOKEN_BLOCK = 512
VMEM_LIMIT = 48 * 1024 * 1024


def _dot(a, b):
    return jnp.dot(a.astype(BF16), b.astype(BF16), preferred_element_type=F32)


def _dot_hi(a, b):
    return jnp.dot(a, b, precision=HIGHEST, preferred_element_type=F32)


def _dot_nt(a, b):
    return lax.dot_general(a.astype(BF16), b.astype(BF16), (((1,), (1,)), ((), ())),
                           preferred_element_type=F32)


def _dot_tn(a, b):
    return lax.dot_general(a.astype(BF16), b.astype(BF16), (((0,), (0,)), ((), ())),
                           preferred_element_type=F32)


def _silu(x):
    return x * (1.0 / (1.0 + jnp.exp(-x)))


def _sigmoid(x):
    return 1.0 / (1.0 + jnp.exp(-x))


def _softplus(x):
    return jnp.maximum(x, 0.0) + jnp.log(1.0 + jnp.exp(-jnp.abs(x)))


def _head_masks(rows):
    lane = lax.broadcasted_iota(I32, (rows, GROUP_W), 1)
    return [(lane >= h * HEAD_DIM) & (lane < (h + 1) * HEAD_DIM) for h in range(N_HEADS)]


def _block_diag_mask():
    r = lax.broadcasted_iota(I32, (GROUP_W, GROUP_W), 0) // HEAD_DIM
    c = lax.broadcasted_iota(I32, (GROUP_W, GROUP_W), 1) // HEAD_DIM
    return r == c


def _params(sem, vmem=VMEM_LIMIT):
    return pltpu.CompilerParams(dimension_semantics=sem, vmem_limit_bytes=vmem)


def _conv_halo(prev_ref, main_ref, next_ref, w_ref, ext_ref, is_first, is_last, width):
    rows = main_ref.shape[0]
    ext_ref[0:HALO, :] = jnp.where(is_first, 0.0, prev_ref[...])
    ext_ref[HALO:HALO + rows, :] = main_ref[...]
    ext_ref[HALO + rows:2 * HALO + rows, :] = jnp.where(is_last, 0.0, next_ref[...])
    pad = (width - 1) // 2
    acc = None
    for k in range(width):
        term = ext_ref[pl.ds(HALO - pad + k, rows), :] * w_ref[k:k + 1, :]
        acc = term if acc is None else acc + term
    return acc


def _dir_chunk(d, c, nc):
    return jnp.where(d == 0, c, nc - 1 - c)


def _inproj_kernel(h_ref, nw_ref, wr_ref, ws_ref, wh_ref, wg_ref, wm_ref,
                   o_ret, o_ssm, o_hy, o_gdn, o_small):
    x = h_ref[...]
    y = x * lax.rsqrt(jnp.mean(x * x, axis=-1, keepdims=True) + EPS) * nw_ref[...]
    yb = y.astype(BF16)
    o_ret[...] = jnp.dot(yb, wr_ref[...], preferred_element_type=F32)
    o_ssm[...] = jnp.dot(yb, ws_ref[...], preferred_element_type=F32)
    o_hy[...] = jnp.dot(yb, wh_ref[...], preferred_element_type=F32)
    o_gdn[...] = jnp.dot(yb, wg_ref[...], preferred_element_type=F32)
    o_small[...] = jnp.dot(yb, wm_ref[...], preferred_element_type=F32)


def _inproj(h, norm_w, w_ret, w_ssm, w_hy, w_gdn, w_small, tm=512):
    L = h.shape[0]
    tm = min(tm, L)
    widths = [w.shape[1] for w in (w_ret, w_ssm, w_hy, w_gdn, w_small)]
    full = lambda w: pl.BlockSpec(w.shape, lambda i: (0, 0))
    return pl.pallas_call(
        _inproj_kernel,
        grid=(L // tm,),
        in_specs=[pl.BlockSpec((tm, D_MODEL), lambda i: (i, 0)), full(norm_w),
                  full(w_ret), full(w_ssm), full(w_hy), full(w_gdn), full(w_small)],
        out_specs=[pl.BlockSpec((tm, n), lambda i: (i, 0)) for n in widths],
        out_shape=[jax.ShapeDtypeStruct((L, n), F32) for n in widths],
        compiler_params=_params(("parallel",)),
        name="inproj",
    )(h, norm_w, w_ret, w_ssm, w_hy, w_gdn, w_small)


def _rope(x, cos, sin_signed, first_half):
    half = HEAD_DIM // 2
    swapped = jnp.where(first_half, pltpu.roll(x, GROUP_W - half, 1), pltpu.roll(x, half, 1))
    return x * cos + swapped * sin_signed


def _ret_kernel(cols_ref, cos_ref, sin_ref, dmask_ref, pwh_ref, pwt_ref, dec_ref, o_ref, s_ref):
    d = pl.program_id(0)
    c = pl.program_id(1)
    T = CHUNK

    @pl.when(c == 0)
    def _():
        s_ref[...] = jnp.zeros_like(s_ref)

    lane = lax.broadcasted_iota(I32, (T, GROUP_W), 1)
    first_half = (lane % HEAD_DIM) < (HEAD_DIM // 2)
    cos = cos_ref[...]
    sin = sin_ref[...]
    q = _rope(cols_ref[:, 0:GROUP_W], cos, sin, first_half)
    k = _rope(cols_ref[:, GROUP_W:2 * GROUP_W], cos, sin, first_half) * (HEAD_DIM ** -0.5)
    v = cols_ref[:, 2 * GROUP_W:3 * GROUP_W]
    pwh = pwh_ref[...]
    pwt = pwt_ref[...]
    bd = _block_diag_mask()
    state = s_ref[...]

    @pl.when(d == 0)
    def _():
        o = _dot(q * pwh, state)
        for h, hm in enumerate(_head_masks(T)):
            s = _dot_nt(jnp.where(hm, q, 0.0), k) * dmask_ref[h]
            o = o + jnp.where(hm, _dot(s, v), 0.0)
        o_ref[...] = o
        s_ref[...] = dec_ref[...] * state + jnp.where(bd, _dot_tn(k * pwt, v), 0.0)

    @pl.when(d == 1)
    def _():
        o_ref[...] = _dot(q * pwt, state)
        s_ref[...] = dec_ref[...] * state + jnp.where(bd, _dot_tn(k * pwh, v), 0.0)


def _retention(cols_ret, cos_t, sin_t, dmask, pwh, pwt, dec):
    L = cols_ret.shape[0]
    nc = L // CHUNK
    cmap = lambda d, c: (_dir_chunk(d, c, nc), 0)
    const2 = lambda a: pl.BlockSpec(a.shape, lambda d, c: (0,) * a.ndim)
    return pl.pallas_call(
        _ret_kernel,
        grid=(2, nc),
        in_specs=[pl.BlockSpec((CHUNK, 4 * GROUP_W), cmap),
                  pl.BlockSpec((CHUNK, GROUP_W), cmap), pl.BlockSpec((CHUNK, GROUP_W), cmap),
                  const2(dmask), const2(pwh), const2(pwt), const2(dec)],
        out_specs=pl.BlockSpec((None, CHUNK, GROUP_W), lambda d, c: (d, _dir_chunk(d, c, nc), 0)),
        out_shape=jax.ShapeDtypeStruct((2, L, GROUP_W), F32),
        scratch_shapes=[pltpu.VMEM((GROUP_W, GROUP_W), F32)],
        compiler_params=_params(("arbitrary", "arbitrary")),
        name="retention",
    )(cols_ret, cos_t, sin_t, dmask, pwh, pwt, dec)


def _ssd_kernel(prev_ref, main_ref, next_ref, small_ref, cw_ref, cb_ref, tri_ref, exp_ref,
                dtb_ref, a_ref, dskip_ref, o_ref, ext_ref, s_ref):
    d = pl.program_id(0)
    c = pl.program_id(1)
    nc = pl.num_programs(1)
    T = CHUNK
    ce = _dir_chunk(d, c, nc)

    @pl.when(c == 0)
    def _():
        s_ref[...] = jnp.zeros_like(s_ref)

    xbc = _conv_halo(prev_ref, main_ref, next_ref, cw_ref, ext_ref, ce == 0, ce == nc - 1, SSM_CONV)
    xbc = _silu(xbc + cb_ref[...])
    xs = xbc[:, 0:GROUP_W]
    tri = tri_ref[...]
    vis = tri > 0.5
    dt = _softplus(_dot_hi(small_ref[...], exp_ref[...]) + dtb_ref[...])
    a_cum = _dot_hi(tri, dt * a_ref[...])
    a_tot = jnp.sum(dt * a_ref[...], axis=0, keepdims=True)
    xdt = xs * dt
    hms = _head_masks(T)
    state = s_ref[...]
    y = jnp.zeros((T, GROUP_W), F32)
    upd = jnp.zeros((SSM_STATE, GROUP_W), F32)
    for g in range(2):
        bm = xbc[:, GROUP_W + g * SSM_STATE:GROUP_W + (g + 1) * SSM_STATE]
        cm = xbc[:, 2 * GROUP_W + g * SSM_STATE:2 * GROUP_W + (g + 1) * SSM_STATE]
        cb = _dot_nt(cm, bm)
        for h in (2 * g, 2 * g + 1):
            hm = hms[h]
            col = jnp.broadcast_to(a_cum[:, h * HEAD_DIM:h * HEAD_DIM + 1], (T, T))
            tot = jnp.broadcast_to(a_tot[:, h * HEAD_DIM:h * HEAD_DIM + 1], (T, T))
            lmask = jnp.where(vis, jnp.exp(col - col.T), 0.0)
            y = y + jnp.where(hm, _dot(cb * lmask, xdt) + _dot(cm * jnp.exp(col), state), 0.0)
            upd = upd + jnp.where(hm[0:SSM_STATE], _dot_tn(bm * jnp.exp(tot - col), xdt), 0.0)
    s_ref[...] = jnp.exp(a_tot) * state + upd
    y = y + jnp.where(d == 0, 1.0, 0.0) * (xs * dskip_ref[...])
    o_ref[...] = y


def _halo_specs(width, nc, nrow8):
    main = pl.BlockSpec((CHUNK, width), lambda d, c: (_dir_chunk(d, c, nc), 0))
    prev = pl.BlockSpec((HALO, width),
                        lambda d, c: (jnp.maximum(_dir_chunk(d, c, nc) * (CHUNK // HALO) - 1, 0), 0))
    nxt = pl.BlockSpec((HALO, width),
                       lambda d, c: (jnp.minimum((_dir_chunk(d, c, nc) + 1) * (CHUNK // HALO), nrow8 - 1), 0))
    return prev, main, nxt


def _ssd(cols_ssm, cols_small, conv_w, conv_b, tri, expand, dt_bias, a_neg, d_skip):
    L = cols_ssm.shape[0]
    nc = L // CHUNK
    prev, main, nxt = _halo_specs(XBC_W, nc, L // HALO)
    cmap = lambda d, c: (_dir_chunk(d, c, nc), 0)
    const = lambda a: pl.BlockSpec(a.shape, lambda d, c: (0,) * a.ndim)
    dirsel = lambda a: pl.BlockSpec((None,) + a.shape[1:], lambda d, c: (d,) + (0,) * (a.ndim - 1))
    return pl.pallas_call(
        _ssd_kernel,
        grid=(2, nc),
        in_specs=[prev, main, nxt, pl.BlockSpec((CHUNK, 128), cmap), const(conv_w), const(conv_b),
                  dirsel(tri), dirsel(expand), dirsel(dt_bias), dirsel(a_neg), const(d_skip)],
        out_specs=pl.BlockSpec((None, CHUNK, GROUP_W), lambda d, c: (d, _dir_chunk(d, c, nc), 0)),
        out_shape=jax.ShapeDtypeStruct((2, L, GROUP_W), F32),
        scratch_shapes=[pltpu.VMEM((CHUNK + 2 * HALO, XBC_W), F32),
                        pltpu.VMEM((SSM_STATE, GROUP_W), F32)],
        compiler_params=_params(("arbitrary", "arbitrary")),
        name="ssd",
    )(cols_ssm, cols_ssm, cols_ssm, cols_small, conv_w, conv_b, tri, expand, dt_bias, a_neg, d_skip)


def _unit_tri_inverse(a, eye, bd16, off_ref):
    n1 = jnp.where(bd16, -a, 0.0)
    t = eye + n1
    p = _dot_hi(n1, n1)
    t = t + _dot_hi(t, p)
    p = _dot_hi(p, p)
    t = t + _dot_hi(t, p)
    p = _dot_hi(p, p)
    t = t + _dot_hi(t, p)
    for lvl in range(3):
        a_off = a * off_ref[lvl]
        t = t - _dot_hi(_dot_hi(t, a_off), t)
    return t


def _gdn_kernel(prev_ref, main_ref, next_ref, small_ref, cw_ref, tri_ref, off_ref, expa_ref,
                expb_ref, dtb_ref, a_ref, o_ref, ext_ref, s_ref):
    d = pl.program_id(0)
    c = pl.program_id(1)
    nc = pl.num_programs(1)
    T = CHUNK
    ce = _dir_chunk(d, c, nc)

    @pl.when(c == 0)
    def _():
        s_ref[...] = jnp.zeros_like(s_ref)

    qkv = _silu(_conv_halo(prev_ref, main_ref, next_ref, cw_ref, ext_ref, ce == 0, ce == nc - 1, GDN_CONV))
    bdf = jnp.where(_block_diag_mask(), 1.0, 0.0)
    q = qkv[:, 0:GROUP_W]
    k = qkv[:, GROUP_W:2 * GROUP_W]
    v = qkv[:, 2 * GROUP_W:3 * GROUP_W]
    q = q * lax.rsqrt(_dot_hi(q * q, bdf) + EPS) * (HEAD_DIM ** -0.5)
    k = k * lax.rsqrt(_dot_hi(k * k, bdf) + EPS)
    small = small_ref[...]
    g = a_ref[...] * _softplus(_dot_hi(small, expa_ref[...]) + dtb_ref[...])
    beta = _sigmoid(_dot_hi(small, expb_ref[...]))
    tri = tri_ref[...]
    vis = tri > 0.5
    ri = lax.broadcasted_iota(I32, (T, T), 0)
    ci = lax.broadcasted_iota(I32, (T, T), 1)
    eye = jnp.where(ri == ci, 1.0, 0.0)
    strict = vis & (ri != ci)
    bd16 = (ri // 16) == (ci // 16)
    gc = _dot_hi(tri, g)
    g_tot = jnp.sum(g, axis=0, keepdims=True)
    kb = k * beta
    vb = v * beta
    kbg = kb * jnp.exp(gc)
    hms = _head_masks(T)
    u = jnp.zeros((T, GROUP_W), F32)
    w = jnp.zeros((T, GROUP_W), F32)
    qks = []
    for h, hm in enumerate(hms):
        col = jnp.broadcast_to(gc[:, h * HEAD_DIM:h * HEAD_DIM + 1], (T, T))
        decay = jnp.where(vis, jnp.exp(col - col.T), 0.0)
        a_h = jnp.where(strict, _dot_nt(jnp.where(hm, kb, 0.0), k) * decay, 0.0)
        tinv = _unit_tri_inverse(a_h, eye, bd16, off_ref)
        u = u + _dot(tinv, jnp.where(hm, vb, 0.0))
        w = w + _dot(tinv, jnp.where(hm, kbg, 0.0))
        qks.append(_dot_nt(jnp.where(hm, q, 0.0), k) * decay)
    state = s_ref[...]
    v_new = u - _dot(w, state)
    o = _dot(q * jnp.exp(gc), state)
    for h, hm in enumerate(hms):
        o = o + _dot(qks[h], jnp.where(hm, v_new, 0.0))
    k_dec = k * jnp.exp(g_tot - gc)
    s_ref[...] = state * jnp.exp(g_tot) + bdf * _dot_tn(k_dec, v_new)
    o_ref[...] = o


def _gdn(cols_gdn, cols_small, conv_w, tri, off, exp_a, exp_b, dt_bias, a_neg):
    L = cols_gdn.shape[0]
    nc = L // CHUNK
    prev, main, nxt = _halo_specs(XBC_W, nc, L // HALO)
    cmap = lambda d, c: (_dir_chunk(d, c, nc), 0)
    const = lambda a: pl.BlockSpec(a.shape, lambda d, c: (0,) * a.ndim)
    dirsel = lambda a: pl.BlockSpec((None,) + a.shape[1:], lambda d, c: (d,) + (0,) * (a.ndim - 1))
    return pl.pallas_call(
        _gdn_kernel,
        grid=(2, nc),
        in_specs=[prev, main, nxt, pl.BlockSpec((CHUNK, 128), cmap), const(conv_w),
                  dirsel(tri), dirsel(off), dirsel(exp_a), dirsel(exp_b), dirsel(dt_bias), dirsel(a_neg)],
        out_specs=pl.BlockSpec((None, CHUNK, GROUP_W), lambda d, c: (d, _dir_chunk(d, c, nc), 0)),
        out_shape=jax.ShapeDtypeStruct((2, L, GROUP_W), F32),
        scratch_shapes=[pltpu.VMEM((CHUNK + 2 * HALO, XBC_W), F32),
                        pltpu.VMEM((GROUP_W, GROUP_W), F32)],
        compiler_params=_params(("arbitrary", "arbitrary")),
        name="gdn",
    )(cols_gdn, cols_gdn, cols_gdn, cols_small, conv_w, tri, off, exp_a, exp_b, dt_bias, a_neg)


def _hy_pre_kernel(prev_ref, main_ref, next_ref, cw_ref, cb_ref, vx_ref, x0_ref, ext_ref):
    i = pl.program_id(0)
    n = pl.num_programs(0)
    u = _conv_halo(prev_ref, main_ref, next_ref, cw_ref, ext_ref, i == 0, i == n - 1, HY_CONV) + cb_ref[...]
    x0_ref[...] = u[:, 0:GROUP_W]
    vx_ref[...] = u[:, 2 * GROUP_W:3 * GROUP_W] * u[:, GROUP_W:2 * GROUP_W]


def _hy_pre(cols_hy, conv_w, conv_b, tm=512):
    L = cols_hy.shape[0]
    tm = min(tm, L)
    nrow8 = L // HALO
    r = tm // HALO
    w = 3 * GROUP_W
    const = lambda a: pl.BlockSpec(a.shape, lambda i: (0,) * a.ndim)
    return pl.pallas_call(
        _hy_pre_kernel,
        grid=(L // tm,),
        in_specs=[pl.BlockSpec((HALO, w), lambda i: (jnp.maximum(i * r - 1, 0), 0)),
                  pl.BlockSpec((tm, w), lambda i: (i, 0)),
                  pl.BlockSpec((HALO, w), lambda i: (jnp.minimum((i + 1) * r, nrow8 - 1), 0)),
                  const(conv_w), const(conv_b)],
        out_specs=[pl.BlockSpec((tm, GROUP_W), lambda i: (i, 0))] * 2,
        out_shape=[jax.ShapeDtypeStruct((L, GROUP_W), F32)] * 2,
        scratch_shapes=[pltpu.VMEM((tm + 2 * HALO, w), F32)],
        compiler_params=_params(("parallel",)),
        name="hyena_pre",
    )(cols_hy, cols_hy, cols_hy, conv_w, conv_b)


def _hy_filter_kernel(z_ref, win_ref, bin_ref, wmid_ref, bmid_ref, freq_ref, wout_ref, delta_ref,
                      f_ref, *, seq_len):
    i = pl.program_id(0)
    tm = z_ref.shape[0]
    z = z_ref[...]
    freq = freq_ref[...]
    h = jnp.sin(freq * (_dot_hi(z, win_ref[...]) + bin_ref[...]))
    for j in range(HY_INNER):
        h = jnp.sin(freq * (_dot_hi(h, wmid_ref[j]) + bmid_ref[j]))
    h = _dot_hi(h, wout_ref[...])
    h = h * (jnp.exp(-z[:, 0:1] * delta_ref[...]) + HY_SHIFT)
    row = i * tm + lax.broadcasted_iota(I32, (tm, 1), 0)
    f_ref[...] = jnp.where(row == seq_len, 0.0, h)


def _hy_filter(z_ext, w_in, b_in, w_mid, b_mid, freq, w_out, abs_delta, tm=512):
    n = z_ext.shape[0]
    L = n // 2
    tm = min(tm, L)
    nblk = n // tm
    const = lambda a: pl.BlockSpec(a.shape, lambda i: (0,) * a.ndim)
    return pl.pallas_call(
        functools.partial(_hy_filter_kernel, seq_len=L),
        grid=(nblk,),
        in_specs=[pl.BlockSpec((tm, HY_EMB_PAD), lambda i: (i, 0)), const(w_in), const(b_in),
                  const(w_mid), const(b_mid), const(freq),
                  pl.BlockSpec((HY_ORDER, GROUP_W), lambda i: (0, i // (nblk // 2))), const(abs_delta)],
        out_specs=pl.BlockSpec((tm, GROUP_W), lambda i: (i, 0)),
        out_shape=jax.ShapeDtypeStruct((n, GROUP_W), F32),
        compiler_params=_params(("parallel",)),
        name="hyena_filter",
    )(z_ext, w_in, b_in, w_mid, b_mid, freq, w_out, abs_delta)


def _left_matmul_kernel(m_ref, x_ref, o_ref):
    o_ref[...] = _dot(m_ref[...], x_ref[...])


def _left_matmul(m, x2d, cb=4096):
    rows, kdim = m.shape
    ncol = x2d.shape[1]
    cb = min(cb, ncol)
    return pl.pallas_call(
        _left_matmul_kernel,
        grid=(ncol // cb,),
        in_specs=[pl.BlockSpec((rows, kdim), lambda i: (0, 0)), pl.BlockSpec((kdim, cb), lambda i: (0, i))],
        out_specs=pl.BlockSpec((rows, cb), lambda i: (0, i)),
        out_shape=jax.ShapeDtypeStruct((rows, ncol), F32),
        compiler_params=_params(("parallel",)),
        name="hyena_outer_dft",
    )(m, x2d)


def _twiddle_inner_dft(a_ref, twc_ref, tws_ref, g_ref):
    ar = a_ref[0]
    ai = a_ref[1]
    cs = twc_ref[...]
    sn = tws_ref[...]
    br = ar * cs + ai * sn
    bi = ai * cs - ar * sn
    x = _dot(g_ref[...], jnp.concatenate([br, bi], axis=0))
    return x[0:FFT_N2], x[FFT_N2:2 * FFT_N2]


def _hy_filter_spec_kernel(a_ref, twc_ref, tws_ref, g_ref, x_ref):
    xr, xi = _twiddle_inner_dft(a_ref, twc_ref, tws_ref, g_ref)
    x_ref[0] = xr
    x_ref[1] = xi


def _hy_conv_spec_kernel(a_ref, twc_ref, tws_ref, g_ref, gi_ref, f_ref, e_ref):
    xr, xi = _twiddle_inner_dft(a_ref, twc_ref, tws_ref, g_ref)
    fr = f_ref[0]
    fi = f_ref[1]
    yr = xr * fr - xi * fi
    yi = xr * fi + xi * fr
    dd = _dot(gi_ref[...], jnp.concatenate([yr, yi], axis=0))
    dr = dd[0:FFT_N2]
    di = dd[FFT_N2:2 * FFT_N2]
    cs = twc_ref[...]
    sn = tws_ref[...]
    e_ref[0] = dr * cs - di * sn
    e_ref[1] = dr * sn + di * cs


def _hy_spec_specs(n1):
    blk = pl.BlockSpec((2, None, FFT_N2, GROUP_W), lambda k: (0, k, 0, 0))
    tw = pl.BlockSpec((None, FFT_N2, 1), lambda k: (k, 0, 0))
    mat = pl.BlockSpec((2 * FFT_N2, 2 * FFT_N2), lambda k: (0, 0))
    return blk, tw, mat


def _hy_filter_spec(a4, twc, tws, g):
    n1 = a4.shape[1]
    blk, tw, mat = _hy_spec_specs(n1)
    return pl.pallas_call(
        _hy_filter_spec_kernel,
        grid=(n1,),
        in_specs=[blk, tw, tw, mat],
        out_specs=blk,
        out_shape=jax.ShapeDtypeStruct(a4.shape, F32),
        compiler_params=_params(("parallel",)),
        name="hyena_filter_spectrum",
    )(a4, twc, tws, g)


def _hy_conv_spec(a4, twc, tws, g, gi, fspec):
    n1 = a4.shape[1]
    blk, tw, mat = _hy_spec_specs(n1)
    return pl.pallas_call(
        _hy_conv_spec_kernel,
        grid=(n1,),
        in_specs=[blk, tw, tw, mat, mat, blk],
        out_specs=blk,
        out_shape=jax.ShapeDtypeStruct(a4.shape, F32),
        compiler_params=_params(("parallel",)),
        name="hyena_spectral_product",
    )(a4, twc, tws, g, gi, fspec)


def _postmix_kernel(h_ref, ret_ref, g_ref, ssd_ref, zs_ref, hy_ref, vx_ref, x0_ref, gdn_ref, zg_ref,
                    gnw_ref, snw_ref, hyb_ref, hnw_ref, dnw_ref, wout_ref, fnw_ref, rw_ref,
                    h1_ref, xn_ref, aff_ref):
    avg = jnp.where(_block_diag_mask(), 1.0 / HEAD_DIM, 0.0)
    o = ret_ref[0] + ret_ref[1]
    mu = _dot_hi(o, avg)
    var = _dot_hi(jnp.square(o - mu), avg)
    m_ret = _silu(g_ref[...]) * ((o - mu) * lax.rsqrt(var + EPS) * gnw_ref[...])

    y = (ssd_ref[0] + ssd_ref[1]) * _silu(zs_ref[...])
    m_ssm = y * lax.rsqrt(jnp.mean(y * y, axis=-1, keepdims=True) + EPS) * snw_ref[...]

    t = (hy_ref[...] + vx_ref[...] * hyb_ref[...]) * x0_ref[...]
    m_hy = t * lax.rsqrt(jnp.mean(t * t, axis=-1, keepdims=True) + EPS) * hnw_ref[...]

    o = gdn_ref[0] + gdn_ref[1]
    m_gdn = o * lax.rsqrt(_dot_hi(o * o, avg) + EPS) * dnw_ref[...] * _silu(zg_ref[...])

    h1 = h_ref[...]
    for gi, m in enumerate((m_ret, m_ssm, m_hy, m_gdn)):
        h1 = h1 + jnp.dot(m.astype(BF16), wout_ref[gi * GROUP_W:(gi + 1) * GROUP_W, :],
                          preferred_element_type=F32)
    h1_ref[...] = h1
    xn = h1 * lax.rsqrt(jnp.mean(h1 * h1, axis=-1, keepdims=True) + EPS) * fnw_ref[...]
    xn_ref[...] = xn.astype(BF16)
    logits = _dot_hi(xn, rw_ref[...])
    logits = logits - jnp.max(logits, axis=-1, keepdims=True)
    ex = jnp.exp(logits)
    aff_ref[...] = ex / jnp.sum(ex, axis=-1, keepdims=True)


def _postmix(h, ret_o, cols_ret, ssd_y, cols_ssm, hy_y, hy_vx, hy_x0, gdn_o, cols_gdn,
             gn_w, ssm_nw, hy_bias, hy_nw, gdn_nw, w_out, ffn_nw, router_w, tm=256):
    L = h.shape[0]
    tm = min(tm, L)
    row = lambda n: pl.BlockSpec((tm, n), lambda i: (i, 0))
    zcol = pl.BlockSpec((tm, GROUP_W), lambda i: (i, 3))
    pair = pl.BlockSpec((2, tm, GROUP_W), lambda i: (0, i, 0))
    const = lambda a: pl.BlockSpec(a.shape, lambda i: (0,) * a.ndim)
    return pl.pallas_call(
        _postmix_kernel,
        grid=(L // tm,),
        in_specs=[row(D_MODEL), pair, zcol, pair, zcol, row(GROUP_W), row(GROUP_W), row(GROUP_W), pair, zcol,
                  const(gn_w), const(ssm_nw), const(hy_bias), const(hy_nw), const(gdn_nw),
                  const(w_out), const(ffn_nw), const(router_w)],
        out_specs=[row(D_MODEL), row(D_MODEL), row(N_EXPERTS)],
        out_shape=[jax.ShapeDtypeStruct((L, D_MODEL), F32), jax.ShapeDtypeStruct((L, D_MODEL), BF16),
                   jax.ShapeDtypeStruct((L, N_EXPERTS), F32)],
        compiler_params=_params(("parallel",)),
        name="postmix",
    )(h, ret_o, cols_ret, ssd_y, cols_ssm, hy_y, hy_vx, hy_x0, gdn_o, cols_gdn,
      gn_w, ssm_nw, hy_bias, hy_nw, gdn_nw, w_out, ffn_nw, router_w)


def _topk_kernel(aff_ref, sel_ref, *, cap, idx_bits):
    bits = lax.bitcast_convert_type(aff_ref[...], I32)
    shape = bits.shape
    capf = float(cap)

    def count(mask):
        return jnp.sum(jnp.where(mask, 1.0, 0.0), axis=1, keepdims=True)

    def value_step(i, thr):
        cand = thr | lax.shift_left(jnp.int32(1), 30 - i)
        return jnp.where(count(bits >= cand) >= capf, cand, thr)

    thr = lax.fori_loop(0, 31, value_step, jnp.zeros((shape[0], 1), I32))
    above = bits > thr
    tied = bits == thr
    need = capf - count(above)
    idx = lax.broadcasted_iota(I32, shape, 1)

    def index_step(i, m):
        cand = m | lax.shift_left(jnp.int32(1), idx_bits - 1 - i)
        return jnp.where(count(tied & (idx < cand)) < need, cand, m)

    last = lax.fori_loop(0, idx_bits, index_step, jnp.zeros((shape[0], 1), I32))
    sel_ref[...] = jnp.where(above | (tied & (idx <= last)), 1.0, 0.0)


def _topk(aff_t, cap):
    n_e, L = aff_t.shape
    return pl.pallas_call(
        functools.partial(_topk_kernel, cap=cap, idx_bits=int(math.log2(L)) + 1),
        out_shape=jax.ShapeDtypeStruct((n_e, L), F32),
        compiler_params=_params(None),
        name="expert_topk",
    )(aff_t)


def _positions_kernel(sel_ref, posm_ref, offs_ref, *, n_e):
    s = sel_ref[...]
    nb = s.shape[0] // n_e
    r = lax.broadcasted_iota(I32, (128, 128), 0)
    c = lax.broadcasted_iota(I32, (128, 128), 1)
    incl = _dot(s, jnp.where(r <= c, 1.0, 0.0))
    tot = _dot(s, jnp.ones((128, 128), F32))
    rb = lax.broadcasted_iota(I32, (nb, nb), 0)
    cbk = lax.broadcasted_iota(I32, (nb, nb), 1)
    before = jnp.where(cbk < rb, 1.0, 0.0)
    for e in range(n_e):
        offs = _dot(before, tot[e * nb:(e + 1) * nb])
        se = s[e * nb:(e + 1) * nb]
        pos = incl[e * nb:(e + 1) * nb] - se + offs
        posm_ref[e * nb:(e + 1) * nb, :] = jnp.where(se > 0.5, pos, -1.0).astype(I32)
        offs_ref[e * nb:(e + 1) * nb, :] = offs.astype(I32)


def _positions(sel2, n_e):
    return pl.pallas_call(
        functools.partial(_positions_kernel, n_e=n_e),
        out_shape=[jax.ShapeDtypeStruct(sel2.shape, I32)] * 2,
        compiler_params=_params(None),
        name="expert_positions",
    )(sel2)


def _one_hot(posm_row, tile_index):
    rows = tile_index * ROW_TILE + lax.broadcasted_iota(I32, (ROW_TILE, posm_row.shape[1]), 0)
    return jnp.where(posm_row == rows, 1.0, 0.0).astype(BF16)


def _gather_kernel(e_ref, j_ref, b_ref, first_ref, valid_ref, posm_ref, x_ref, o_ref):
    w = pl.program_id(0)

    @pl.when(first_ref[w] == 1)
    def _():
        o_ref[...] = jnp.zeros_like(o_ref)

    @pl.when(valid_ref[w] == 1)
    def _():
        p = _one_hot(posm_ref[...], j_ref[w])
        o_ref[...] += jnp.dot(p, x_ref[...], preferred_element_type=F32).astype(BF16)


def _gather(work, posm4, xn, cap):
    n_e, nblk = posm4.shape[0], posm4.shape[1]
    tb = posm4.shape[3]
    n_work = work[0].shape[0]
    return pl.pallas_call(
        _gather_kernel,
        grid_spec=pltpu.PrefetchScalarGridSpec(
            num_scalar_prefetch=5,
            grid=(n_work,),
            in_specs=[pl.BlockSpec((None, None, 1, tb), lambda w, e, j, b, f, v: (e[w], b[w], 0, 0)),
                      pl.BlockSpec((tb, D_MODEL), lambda w, e, j, b, f, v: (b[w], 0))],
            out_specs=pl.BlockSpec((None, ROW_TILE, D_MODEL), lambda w, e, j, b, f, v: (e[w], j[w], 0)),
        ),
        out_shape=jax.ShapeDtypeStruct((n_e, cap, D_MODEL), BF16),
        compiler_params=_params(("arbitrary",)),
        name="expert_gather",
    )(*work, posm4, xn)


def _ffn_kernel(x_ref, wg_ref, wu_ref, wd_ref, o_ref, acc_ref):
    f = pl.program_id(1)

    @pl.when(f == 0)
    def _():
        acc_ref[...] = jnp.zeros_like(acc_ref)

    x = x_ref[...]
    gate = jnp.dot(x, wg_ref[...].astype(BF16), preferred_element_type=F32)
    up = jnp.dot(x, wu_ref[...].astype(BF16), preferred_element_type=F32)
    acc_ref[...] += _dot(_silu(gate) * up, wd_ref[...])

    @pl.when(f == pl.num_programs(1) - 1)
    def _():
        o_ref[...] = acc_ref[...].astype(BF16)


def _expert_ffn(xs, w_gate, w_up, w_down, fb=256):
    n_e, cap, _ = xs.shape
    return pl.pallas_call(
        _ffn_kernel,
        grid=(n_e, EXPERT_FF // fb),
        in_specs=[pl.BlockSpec((None, cap, D_MODEL), lambda e, f: (e, 0, 0)),
                  pl.BlockSpec((None, D_MODEL, fb), lambda e, f: (e, 0, f)),
                  pl.BlockSpec((None, D_MODEL, fb), lambda e, f: (e, 0, f)),
                  pl.BlockSpec((None, fb, D_MODEL), lambda e, f: (e, f, 0))],
        out_specs=pl.BlockSpec((None, cap, D_MODEL), lambda e, f: (e, 0, 0)),
        out_shape=jax.ShapeDtypeStruct((n_e, cap, D_MODEL), BF16),
        scratch_shapes=[pltpu.VMEM((cap, D_MODEL), F32)],
        compiler_params=_params(("parallel", "arbitrary")),
        name="expert_ffn",
    )(xs, w_gate, w_up, w_down)


def _scatter_kernel(e_ref, j_ref, b_ref, first_ref, valid_ref, posm_ref, y_ref, aff_ref, o_ref):
    w = pl.program_id(0)

    @pl.when(first_ref[w] == 1)
    def _():
        o_ref[...] = jnp.zeros_like(o_ref)

    @pl.when(valid_ref[w] == 1)
    def _():
        p = _one_hot(posm_ref[...], j_ref[w])
        contrib = lax.dot_general(p, y_ref[...], (((0,), (0,)), ((), ())), preferred_element_type=F32)
        lane = lax.broadcasted_iota(I32, aff_ref.shape, 1)
        gate = jnp.sum(jnp.where(lane == e_ref[w], aff_ref[...], 0.0), axis=1, keepdims=True)
        o_ref[...] += gate * contrib


def _scatter(work, posm4, ye, aff):
    tb = posm4.shape[3]
    L = aff.shape[0]
    n_work = work[0].shape[0]
    return pl.pallas_call(
        _scatter_kernel,
        grid_spec=pltpu.PrefetchScalarGridSpec(
            num_scalar_prefetch=5,
            grid=(n_work,),
            in_specs=[pl.BlockSpec((None, None, 1, tb), lambda w, e, j, b, f, v: (e[w], b[w], 0, 0)),
                      pl.BlockSpec((None, ROW_TILE, D_MODEL), lambda w, e, j, b, f, v: (e[w], j[w], 0)),
                      pl.BlockSpec((tb, N_EXPERTS), lambda w, e, j, b, f, v: (b[w], 0))],
            out_specs=pl.BlockSpec((tb, D_MODEL), lambda w, e, j, b, f, v: (b[w], 0)),
        ),
        out_shape=jax.ShapeDtypeStruct((L, D_MODEL), F32),
        compiler_params=_params(("arbitrary",)),
        name="expert_scatter",
    )(*work, posm4, ye, aff)


def _work_lists(offs, cap, tb):
    n_e, nb128 = offs.shape
    per = tb // 128
    nblk = nb128 // per
    ntile = cap // ROW_TILE
    start = offs[:, ::per]
    end = jnp.concatenate([start[:, 1:], jnp.full((n_e, 1), cap, I32)], axis=1)
    cnt = end - start
    first_tile = jnp.minimum(start // ROW_TILE, ntile - 1)
    last_tile = jnp.where(cnt > 0, (end - 1) // ROW_TILE, first_tile)
    n_items = jnp.where(cnt > 0, last_tile - first_tile + 1, 0)
    n_items = n_items.at[0].set(jnp.maximum(n_items[0], 1))
    n_work = n_e * (ntile + nblk - 1) + nblk
    flat = n_items.T.reshape(-1)
    ends = jnp.cumsum(flat)
    total = ends[-1]
    w = jnp.arange(n_work, dtype=I32)
    wc = jnp.minimum(w, total - 1)
    pair = jnp.searchsorted(ends, wc, side="right").astype(I32)
    b = pair // n_e
    e = pair % n_e
    j = first_tile.T.reshape(-1)[pair] + (wc - (ends[pair] - flat[pair]))
    valid = (w < total).astype(I32)

    def with_first(key, order):
        eo, jo, bo, vo, ko = e[order], j[order], b[order], valid[order], key[order]
        first = jnp.concatenate([jnp.ones((1,), I32), (ko[1:] != ko[:-1]).astype(I32)])
        return eo, jo, bo, first, vo

    ident = jnp.arange(n_work)
    scatter_work = with_first(b, ident)
    tile_key = (e * ntile + j) * 2 + (1 - valid)
    order = jnp.argsort(tile_key * n_work + w)
    gather_work = with_first((e * ntile + j)[...], order)
    return gather_work, scatter_work


def _ple_kernel(h_ref, moe_ref, p_ref, nw_ref, wg_ref, wp_ref, fw_ref, o_ref, *, final):
    h2 = h_ref[...] + moe_ref[...]
    hn = h2 * lax.rsqrt(jnp.mean(h2 * h2, axis=-1, keepdims=True) + EPS) * nw_ref[...]
    gate = _sigmoid(jnp.dot(hn.astype(BF16), wg_ref[...], preferred_element_type=F32))
    h3 = h2 + gate * jnp.dot(p_ref[...].astype(BF16), wp_ref[...], preferred_element_type=F32)
    if final:
        h3 = h3 * lax.rsqrt(jnp.mean(h3 * h3, axis=-1, keepdims=True) + EPS) * fw_ref[...]
    o_ref[...] = h3


def _ple(h1, moe, p, norm_w, w_gate, w_proj, final_w, final, tm=512):
    L = h1.shape[0]
    tm = min(tm, L)
    row = lambda n: pl.BlockSpec((tm, n), lambda i: (i, 0))
    const = lambda a: pl.BlockSpec(a.shape, lambda i: (0,) * a.ndim)
    return pl.pallas_call(
        functools.partial(_ple_kernel, final=final),
        grid=(L // tm,),
        in_specs=[row(D_MODEL), row(D_MODEL), row(PLE_DIM), const(norm_w), const(w_gate), const(w_proj),
                  const(final_w)],
        out_specs=row(D_MODEL),
        out_shape=jax.ShapeDtypeStruct((L, D_MODEL), F32),
        compiler_params=_params(("parallel",)),
        name="ple",
    )(h1, moe, p, norm_w, w_gate, w_proj, final_w)


def _expand_table(base):
    t = np.zeros((2, 128, GROUP_W), np.float32)
    for d in range(2):
        for h in range(N_HEADS):
            t[d, base + N_HEADS * d + h, h * HEAD_DIM:(h + 1) * HEAD_DIM] = 1.0
    return t


def _direction_tables():
    i = np.arange(CHUNK)
    lower = (i[:, None] >= i[None, :]).astype(np.float32)
    tri = np.stack([lower, lower.T])
    off = np.zeros((2, 3, CHUNK, CHUNK), np.float32)
    for lvl, s in enumerate((16, 32, 64)):
        same_pair = (i[:, None] // (2 * s)) == (i[None, :] // (2 * s))
        lo = same_pair & ((i[:, None] // s) % 2 == 1) & ((i[None, :] // s) % 2 == 0)
        off[0, lvl] = lo
        off[1, lvl] = lo.T
    return tri, off


def _retention_tables():
    hh = np.arange(N_HEADS, dtype=np.float64)
    lg = np.log(1.0 - 2.0 ** (-5.0 - hh))
    t = np.arange(CHUNK, dtype=np.float64)
    dmask = np.exp(np.abs(t[:, None] - t[None, :])[None] * lg[:, None, None])
    lanes = np.repeat(lg, HEAD_DIM)[None, :]
    pwt = np.exp((CHUNK - t)[:, None] * lanes)
    pwh = np.exp(t[:, None] * lanes)
    dec = np.exp(CHUNK * lanes)
    return [a.astype(np.float32) for a in (dmask, pwh, pwt, dec)]


def _dft_tables(n1):
    n = n1 * FFT_N2
    k1 = np.arange(n1, dtype=np.float64)
    ang1 = 2.0 * np.pi * np.outer(k1, k1) / n1
    c1, s1 = np.cos(ang1), np.sin(ang1)
    f1 = np.concatenate([c1, -s1], axis=0)
    f3 = np.concatenate([c1[:n1 // 2], -s1[:n1 // 2]], axis=1) / n
    k2 = np.arange(FFT_N2, dtype=np.float64)
    ang2 = 2.0 * np.pi * np.outer(k2, k2) / FFT_N2
    c2, s2 = np.cos(ang2), np.sin(ang2)
    g = np.block([[c2, s2], [-s2, c2]])
    gi = np.block([[c2, -s2], [s2, c2]])
    angt = 2.0 * np.pi * np.outer(k1, k2) / n
    twc, tws = np.cos(angt)[:, :, None], np.sin(angt)[:, :, None]
    return [a.astype(np.float32) for a in (f1, f3, g, gi, twc, tws)]


def _hyena_positions(L):
    t = jnp.linspace(0.0, 1.0, L, dtype=F32)[:, None]
    bands = jnp.linspace(1e-4, HY_BANDS - 1, HY_BANDS, dtype=F32)
    ang = (2.0 * math.pi / L) * jnp.arange(L, dtype=F32)[:, None] * bands[None]
    z = jnp.concatenate([t, jnp.cos(ang), -jnp.sin(ang), jnp.zeros((L, HY_EMB_PAD - HY_EMB), F32)], axis=-1)
    return jnp.concatenate([z, z[0:1], z[::-1][0:L - 1]], axis=0)


def _rope_tables(L):
    inv = ROPE_BASE ** (-jnp.arange(0, HEAD_DIM, 2, dtype=F32) / HEAD_DIM)
    ang = jnp.arange(L, dtype=F32)[:, None] * inv[None]
    cos, sin = jnp.cos(ang), jnp.sin(ang)
    cos_t = jnp.tile(jnp.concatenate([cos, cos], axis=-1), (1, N_HEADS))
    sin_t = jnp.tile(jnp.concatenate([-sin, sin], axis=-1), (1, N_HEADS))
    return cos_t, sin_t


def _lanes(v):
    return jnp.repeat(v, HEAD_DIM, axis=-1)[..., None, :]


def kernel(x, p, norm_mix_w, w_in, ret_gn_w, ssm_conv_w, ssm_conv_b, ssm_a_log, ssm_dt_bias, ssm_d, ssm_norm_w, hy_conv_w, hy_conv_b, hy_filt_w_in, hy_filt_b_in, hy_filt_w_mid, hy_filt_b_mid, hy_filt_freq, hy_filt_w_out, hy_bias, hy_norm_w, gdn_conv_w, gdn_a_log, gdn_dt_bias, gdn_norm_w, w_out, norm_ffn_w, router_w, exp_w_gate, exp_w_up, exp_w_down, ple_norm_w, ple_gate_w, ple_proj_w, final_norm_w):
    batch, L, _ = x.shape
    depth = w_in.shape[0]
    assert batch == 1 and L % TOKEN_BLOCK == 0 and (2 * L) % (8 * FFT_N2) == 0
    cap = CAPACITY_FACTOR * L // N_EXPERTS
    assert cap % ROW_TILE == 0
    n1 = 2 * L // FFT_N2

    tri, off = (jnp.asarray(a) for a in _direction_tables())
    dmask, pwh, pwt, dec = (jnp.asarray(a) for a in _retention_tables())
    f1, f3, g_mat, gi_mat, twc, tws = (jnp.asarray(a) for a in _dft_tables(n1))
    exp_dt, exp_a, exp_b = (jnp.asarray(_expand_table(b)) for b in (0, 8, 16))
    cos_t, sin_t = _rope_tables(L)
    z_ext = _hyena_positions(L)
    max_decay = math.log(HY_DECAY_TARGET) / HY_FAST_PCT
    min_decay = math.log(HY_DECAY_TARGET) / HY_SLOW_PCT
    abs_delta = jnp.abs(jnp.linspace(min_decay, max_decay, GROUP_W, dtype=F32))[None, :]
    row = lambda v: v[None, :]

    h = x[0]
    for i in range(depth):
        w = w_in[i]
        o_ssm = 4 * GROUP_W
        o_hy = o_ssm + GROUP_W + XBC_W + 2 * N_HEADS
        o_gdn = o_hy + 3 * GROUP_W
        o_ab = o_gdn + 4 * GROUP_W
        w_ret = w[:, 0:o_ssm].astype(BF16)
        w_ssm = jnp.concatenate([w[:, o_ssm + GROUP_W:o_ssm + GROUP_W + XBC_W], w[:, o_ssm:o_ssm + GROUP_W]],
                                axis=1).astype(BF16)
        w_hy = w[:, o_hy:o_gdn].astype(BF16)
        w_gdn = w[:, o_gdn:o_ab].astype(BF16)
        w_small = jnp.concatenate([w[:, o_hy - 2 * N_HEADS:o_hy], w[:, o_ab:o_ab + 4 * N_HEADS],
                                   jnp.zeros((D_MODEL, 128 - 6 * N_HEADS), F32)], axis=1).astype(BF16)
        c_ret, c_ssm, c_hy, c_gdn, c_small = _inproj(h, row(norm_mix_w[i]), w_ret, w_ssm, w_hy, w_gdn, w_small)

        ret_o = _retention(c_ret, cos_t, sin_t, dmask, pwh, pwt, dec)

        ssd_y = _ssd(c_ssm, c_small, ssm_conv_w[i], row(ssm_conv_b[i]), tri, exp_dt,
                     _lanes(ssm_dt_bias[i]), _lanes(-jnp.exp(ssm_a_log[i])), _lanes(ssm_d[i]))

        gdn_o = _gdn(c_gdn, c_small, gdn_conv_w[i], tri, off, exp_a, exp_b,
                     _lanes(gdn_dt_bias[i]), _lanes(-jnp.exp(gdn_a_log[i])))

        hy_vx, hy_x0 = _hy_pre(c_hy, hy_conv_w[i], row(hy_conv_b[i]))
        w_in_pad = jnp.concatenate([hy_filt_w_in[i], jnp.zeros((HY_EMB_PAD - HY_EMB, HY_ORDER), F32)], axis=0)
        filt = _hy_filter(z_ext, w_in_pad, row(hy_filt_b_in[i]), hy_filt_w_mid[i], hy_filt_b_mid[i][:, None, :],
                          row(hy_filt_freq[i]), hy_filt_w_out[i], abs_delta)
        spec_shape = (2, n1, FFT_N2, GROUP_W)
        f_a = _left_matmul(f1, filt.reshape(n1, FFT_N2 * GROUP_W)).reshape(spec_shape)
        f_spec = _hy_filter_spec(f_a, twc, tws, g_mat)
        v_a = _left_matmul(f1[:, :n1 // 2], hy_vx.reshape(n1 // 2, FFT_N2 * GROUP_W)).reshape(spec_shape)
        e_spec = _hy_conv_spec(v_a, twc, tws, g_mat, gi_mat, f_spec)
        hy_y = _left_matmul(f3, e_spec.reshape(2 * n1, FFT_N2 * GROUP_W)).reshape(L, GROUP_W)

        h1, xn, aff = _postmix(h, ret_o, c_ret, ssd_y, c_ssm, hy_y, hy_vx, hy_x0, gdn_o, c_gdn,
                               row(ret_gn_w[i]), row(ssm_norm_w[i]), row(hy_bias[i]), row(hy_norm_w[i]),
                               row(jnp.tile(gdn_norm_w[i], N_HEADS)), w_out[i].astype(BF16),
                               row(norm_ffn_w[i]), router_w[i])

        sel = _topk(aff.T, cap)
        posm, offs = _positions(sel.reshape(N_EXPERTS * (L // 128), 128), N_EXPERTS)
        posm4 = posm.reshape(N_EXPERTS, L // TOKEN_BLOCK, 1, TOKEN_BLOCK)
        gather_work, scatter_work = _work_lists(offs[:, 0].reshape(N_EXPERTS, L // 128), cap, TOKEN_BLOCK)
        xs = _gather(gather_work, posm4, xn, cap)
        ye = _expert_ffn(xs, exp_w_gate[i], exp_w_up[i], exp_w_down[i])
        moe = _scatter(scatter_work, posm4, ye, aff)

        h = _ple(h1, moe, p[i, 0], row(ple_norm_w[i]), ple_gate_w[i].astype(BF16), ple_proj_w[i].astype(BF16),
                 row(final_norm_w), final=(i == depth - 1))
    return h[None]
```

```python
import functools
import math

import numpy as np
import jax
import jax.numpy as jnp
from jax import lax
from jax.experimental import pallas as pl
from jax.experimental.pallas import tpu as pltpu

F32 = jnp.float32
BF16 = jnp.bfloat16
I32 = jnp.int32
HIGHEST = lax.Precision.HIGHEST

D_MODEL = 1024
GROUP_W = 256
HEAD_DIM = 64
N_HEADS = GROUP_W // HEAD_DIM
CHUNK = 128
SCAN_CHUNKS = 2
EPS = 1e-6
ROPE_BASE = 10000.0
SSM_STATE = 128
SSM_CONV = 5
XBC_W = 3 * GROUP_W
HY_CONV = 3
HY_EMB = 33
HY_EMB_PAD = 64
HY_BANDS = (HY_EMB - 1) // 2
HY_ORDER = 64
HY_INNER = 2
HY_DECAY_TARGET = 1e-2
HY_FAST_PCT = 0.3
HY_SLOW_PCT = 1.5
HY_SHIFT = 0.05
GDN_CONV = 5
N_EXPERTS = 16
CAPACITY_FACTOR = 2
EXPERT_FF = 1024
PLE_DIM = 256

HALO = 8
FFT_N2 = 256
ROW_TILE = 128
TOKEN_BLOCK = 512
EXPERT_GROUPS = 2
VMEM_LIMIT = 48 * 1024 * 1024
VMEM_LIMIT_RESIDENT = 58 * 1024 * 1024


def _dot(a, b):
    return jnp.dot(a.astype(BF16), b.astype(BF16), preferred_element_type=F32)


def _dot_hi(a, b):
    return jnp.dot(a, b, precision=HIGHEST, preferred_element_type=F32)


def _dot_split(a, b, terms):
    bb = b.astype(BF16)
    out, rest = None, a
    for _ in range(terms):
        piece = rest.astype(BF16)
        part = jnp.dot(piece, bb, preferred_element_type=F32)
        out = part if out is None else out + part
        rest = rest - piece.astype(F32)
    return out


def _dot_split_rhs(a, b, terms):
    ab = a.astype(BF16)
    out, rest = None, b
    for _ in range(terms):
        piece = rest.astype(BF16)
        part = jnp.dot(ab, piece, preferred_element_type=F32)
        out = part if out is None else out + part
        rest = rest - piece.astype(F32)
    return out


def _dot_nt(a, b):
    return lax.dot_general(a.astype(BF16), b.astype(BF16), (((1,), (1,)), ((), ())),
                           preferred_element_type=F32)


def _dot_tn(a, b):
    return lax.dot_general(a.astype(BF16), b.astype(BF16), (((0,), (0,)), ((), ())),
                           preferred_element_type=F32)


def _silu(x):
    return x * (1.0 / (1.0 + jnp.exp(-x)))


def _sigmoid(x):
    return 1.0 / (1.0 + jnp.exp(-x))


def _softplus(x):
    return jnp.maximum(x, 0.0) + jnp.log(1.0 + jnp.exp(-jnp.abs(x)))


def _head_masks(rows):
    lane = lax.broadcasted_iota(I32, (rows, GROUP_W), 1)
    return [(lane >= h * HEAD_DIM) & (lane < (h + 1) * HEAD_DIM) for h in range(N_HEADS)]


def _stack_heads(x, hms):
    return jnp.concatenate([jnp.where(hm, x, 0.0) for hm in hms], axis=0)


def _pair_diag(x):
    t = x.shape[0]
    left = lax.broadcasted_iota(I32, x.shape, 1) < t
    return jnp.concatenate([jnp.where(left, x, 0.0), jnp.where(left, 0.0, x)], axis=0)


def _block_diag_mask():
    r = lax.broadcasted_iota(I32, (GROUP_W, GROUP_W), 0) // HEAD_DIM
    c = lax.broadcasted_iota(I32, (GROUP_W, GROUP_W), 1) // HEAD_DIM
    return r == c


def _params(sem, vmem=VMEM_LIMIT):
    return pltpu.CompilerParams(dimension_semantics=sem, vmem_limit_bytes=vmem)


def _conv_halo(prev_ref, main_ref, next_ref, w_ref, ext_ref, is_first, is_last, width):
    rows = main_ref.shape[0]
    ext_ref[0:HALO, :] = jnp.where(is_first, 0.0, prev_ref[...])
    ext_ref[HALO:HALO + rows, :] = main_ref[...]
    ext_ref[HALO + rows:2 * HALO + rows, :] = jnp.where(is_last, 0.0, next_ref[...])
    pad = (width - 1) // 2
    acc = None
    for k in range(width):
        term = ext_ref[pl.ds(HALO - pad + k, rows), :] * w_ref[k:k + 1, :]
        acc = term if acc is None else acc + term
    return acc


def _halo_specs(rows, width, nblocks, reverse):
    per = rows // HALO
    last8 = nblocks * per - 1
    blk = (lambda i: nblocks - 1 - i) if reverse else (lambda i: i)
    prev = pl.BlockSpec((HALO, width), lambda i: (jnp.maximum(blk(i) * per - 1, 0), 0))
    main = pl.BlockSpec((rows, width), lambda i: (blk(i), 0))
    nxt = pl.BlockSpec((HALO, width), lambda i: (jnp.minimum((blk(i) + 1) * per, last8), 0))
    return prev, main, nxt, blk


def _scan_masks(reverse):
    ri = lax.broadcasted_iota(I32, (CHUNK, CHUNK), 0)
    ci = lax.broadcasted_iota(I32, (CHUNK, CHUNK), 1)
    vis = (ci >= ri) if reverse else (ci <= ri)
    return ri, ci, vis, jnp.where(vis, 1.0, 0.0)


def _inproj_kernel(h_ref, nw_ref, wr_ref, ws_ref, wh_ref, wg_ref, wm_ref,
                   o_ret, o_ssm, o_hy, o_gdn, o_small):
    x = h_ref[...]
    y = x * lax.rsqrt(jnp.mean(x * x, axis=-1, keepdims=True) + EPS) * nw_ref[...]
    yb = y.astype(BF16)
    o_ret[...] = jnp.dot(yb, wr_ref[...], preferred_element_type=F32)
    o_ssm[...] = jnp.dot(yb, ws_ref[...], preferred_element_type=F32)
    o_hy[...] = jnp.dot(yb, wh_ref[...], preferred_element_type=F32)
    o_gdn[...] = jnp.dot(yb, wg_ref[...], preferred_element_type=F32)
    o_small[...] = jnp.dot(yb, wm_ref[...], preferred_element_type=F32)


def _inproj(h, norm_w, w_ret, w_ssm, w_hy, w_gdn, w_small, tm=512):
    L = h.shape[0]
    tm = min(tm, L)
    widths = [w.shape[1] for w in (w_ret, w_ssm, w_hy, w_gdn, w_small)]
    full = lambda w: pl.BlockSpec(w.shape, lambda i: (0, 0))
    return pl.pallas_call(
        _inproj_kernel,
        grid=(L // tm,),
        in_specs=[pl.BlockSpec((tm, D_MODEL), lambda i: (i, 0)), full(norm_w),
                  full(w_ret), full(w_ssm), full(w_hy), full(w_gdn), full(w_small)],
        out_specs=[pl.BlockSpec((tm, n), lambda i: (i, 0)) for n in widths],
        out_shape=[jax.ShapeDtypeStruct((L, n), F32) for n in widths],
        compiler_params=_params(("parallel",)),
        name="inproj",
    )(h, norm_w, w_ret, w_ssm, w_hy, w_gdn, w_small)


def _rope(x, cos, sin_signed, first_half):
    half = HEAD_DIM // 2
    swapped = jnp.where(first_half, pltpu.roll(x, GROUP_W - half, 1), pltpu.roll(x, half, 1))
    return x * cos + swapped * sin_signed


def _ret_kernel(cols_ref, cos_ref, sin_ref, dmask_ref, pwh_ref, pwt_ref, dec_ref, o_ref, s_ref):
    d = pl.program_id(0)
    c = pl.program_id(1)
    T = CHUNK

    @pl.when(c == 0)
    def _():
        s_ref[...] = jnp.zeros_like(s_ref)

    lane = lax.broadcasted_iota(I32, (T, GROUP_W), 1)
    first_half = (lane % HEAD_DIM) < (HEAD_DIM // 2)
    cos = cos_ref[...]
    sin = sin_ref[...]
    q = _rope(cols_ref[:, 0:GROUP_W], cos, sin, first_half)
    k = _rope(cols_ref[:, GROUP_W:2 * GROUP_W], cos, sin, first_half) * (HEAD_DIM ** -0.5)
    v = cols_ref[:, 2 * GROUP_W:3 * GROUP_W]
    pwh = pwh_ref[...]
    pwt = pwt_ref[...]
    bd = _block_diag_mask()
    state = s_ref[...]

    @pl.when(d == 0)
    def _():
        hms = _head_masks(T)
        s_all = _dot_nt(_stack_heads(q, hms), k)
        s_cat = jnp.concatenate([s_all[h * T:(h + 1) * T] * dmask_ref[h] for h in range(N_HEADS)], axis=1)
        o_ref[...] = _dot(q * pwh, state) + _dot(s_cat, _stack_heads(v, hms))
        s_ref[...] = dec_ref[...] * state + jnp.where(bd, _dot_tn(k * pwt, v), 0.0)

    @pl.when(d == 1)
    def _():
        o_ref[...] = _dot(q * pwt, state)
        s_ref[...] = dec_ref[...] * state + jnp.where(bd, _dot_tn(k * pwh, v), 0.0)


def _dir_chunk(d, c, nc):
    return jnp.where(d == 0, c, nc - 1 - c)


def _retention(cols_ret, cos_t, sin_t, dmask, pwh, pwt, dec):
    L = cols_ret.shape[0]
    nc = L // CHUNK
    cmap = lambda d, c: (_dir_chunk(d, c, nc), 0)
    const2 = lambda a: pl.BlockSpec(a.shape, lambda d, c: (0,) * a.ndim)
    return pl.pallas_call(
        _ret_kernel,
        grid=(2, nc),
        in_specs=[pl.BlockSpec((CHUNK, 4 * GROUP_W), cmap),
                  pl.BlockSpec((CHUNK, GROUP_W), cmap), pl.BlockSpec((CHUNK, GROUP_W), cmap),
                  const2(dmask), const2(pwh), const2(pwt), const2(dec)],
        out_specs=pl.BlockSpec((None, CHUNK, GROUP_W), lambda d, c: (d, _dir_chunk(d, c, nc), 0)),
        out_shape=jax.ShapeDtypeStruct((2, L, GROUP_W), F32),
        scratch_shapes=[pltpu.VMEM((GROUP_W, GROUP_W), F32)],
        compiler_params=_params(("arbitrary", "arbitrary")),
        name="retention",
    )(cols_ret, cos_t, sin_t, dmask, pwh, pwt, dec)


def _ssd_kernel(prev_ref, main_ref, next_ref, small_ref, cw_ref, cb_ref, exp_ref,
                dtb_ref, a_ref, dskip_ref, o_ref, ext_ref, s_ref, *, reverse):
    i = pl.program_id(0)
    n = pl.num_programs(0)
    blk = (n - 1 - i) if reverse else i
    T = CHUNK

    @pl.when(i == 0)
    def _():
        s_ref[...] = jnp.zeros_like(s_ref)

    xbc = _conv_halo(prev_ref, main_ref, next_ref, cw_ref, ext_ref, blk == 0, blk == n - 1, SSM_CONV)
    xbc = _silu(xbc + cb_ref[...])
    xs = xbc[:, 0:GROUP_W]
    dt = _softplus(_dot_split(small_ref[...], exp_ref[...], 3) + dtb_ref[...])
    a = dt * a_ref[...]
    xdt = xs * dt
    _, _, vis, trif = _scan_masks(reverse)
    hms = _head_masks(T)
    prep = []
    for s in range(SCAN_CHUNKS):
        r0, r1 = s * T, (s + 1) * T
        a_cum = _dot_split_rhs(trif, a[r0:r1], 3)
        a_tot = jnp.sum(a[r0:r1], axis=0, keepdims=True)
        xdt_s = xdt[r0:r1]
        scores, c_exp, b_dec = [], [], []
        for g in range(2):
            bm = xbc[r0:r1, GROUP_W + g * SSM_STATE:GROUP_W + (g + 1) * SSM_STATE]
            cm = xbc[r0:r1, 2 * GROUP_W + g * SSM_STATE:2 * GROUP_W + (g + 1) * SSM_STATE]
            cb = _dot_nt(cm, bm)
            for h in (2 * g, 2 * g + 1):
                col = jnp.broadcast_to(a_cum[:, h * HEAD_DIM:h * HEAD_DIM + 1], (T, T))
                tot = jnp.broadcast_to(a_tot[:, h * HEAD_DIM:h * HEAD_DIM + 1], (T, T))
                scores.append(cb * jnp.where(vis, jnp.exp(col - col.T), 0.0))
                c_exp.append(cm * jnp.exp(col))
                b_dec.append(bm * jnp.exp(tot - col))
        y_in = _dot(jnp.concatenate(scores, axis=1), _stack_heads(xdt_s, hms))
        prep.append((y_in, jnp.concatenate(c_exp, axis=1), jnp.concatenate(b_dec, axis=1), xdt_s,
                     jnp.exp(a_tot)))
    state = s_ref[...]
    for s in (reversed(range(SCAN_CHUNKS)) if reverse else range(SCAN_CHUNKS)):
        y, c_cat, b_cat, xdt_s, decay = prep[s]
        y = y + _dot(c_cat, _stack_heads(state, hms))
        upd_all = _dot_tn(b_cat, xdt_s)
        upd = jnp.zeros((SSM_STATE, GROUP_W), F32)
        for h in range(N_HEADS):
            upd = upd + jnp.where(hms[h], upd_all[h * SSM_STATE:(h + 1) * SSM_STATE], 0.0)
        state = decay * state + upd
        if not reverse:
            y = y + xs[s * T:(s + 1) * T] * dskip_ref[...]
        o_ref[s * T:(s + 1) * T, :] = y
    s_ref[...] = state


def _ssd(cols_ssm, cols_small, conv_w, conv_b, expand, dt_bias, a_neg, d_skip, reverse):
    L = cols_ssm.shape[0]
    rows = SCAN_CHUNKS * CHUNK
    nblk = L // rows
    prev, main, nxt, blk = _halo_specs(rows, XBC_W, nblk, reverse)
    const = lambda a: pl.BlockSpec(a.shape, lambda i: (0,) * a.ndim)
    return pl.pallas_call(
        functools.partial(_ssd_kernel, reverse=reverse),
        grid=(nblk,),
        in_specs=[prev, main, nxt, pl.BlockSpec((rows, 128), lambda i: (blk(i), 0)), const(conv_w),
                  const(conv_b), const(expand), const(dt_bias), const(a_neg), const(d_skip)],
        out_specs=pl.BlockSpec((rows, GROUP_W), lambda i: (blk(i), 0)),
        out_shape=jax.ShapeDtypeStruct((L, GROUP_W), F32),
        scratch_shapes=[pltpu.VMEM((rows + 2 * HALO, XBC_W), F32),
                        pltpu.VMEM((SSM_STATE, GROUP_W), F32)],
        compiler_params=_params(("arbitrary",)),
        name="ssd_bwd" if reverse else "ssd_fwd",
    )(cols_ssm, cols_ssm, cols_ssm, cols_small, conv_w, conv_b, expand, dt_bias, a_neg, d_skip)


def _unit_tri_inverse_pairs(a_list, eye2, bd16, offs):
    n1 = [jnp.where(bd16, -a, 0.0) for a in a_list]
    t = [eye2 + x for x in n1]
    p = [_dot(x, _pair_diag(x)) for x in n1]
    for lvl in range(3):
        pd = [_pair_diag(x) for x in p]
        t = [x + _dot(x, d) for x, d in zip(t, pd)]
        if lvl < 2:
            p = [_dot(x, d) for x, d in zip(p, pd)]
    for off in offs:
        m = [_dot(x, _pair_diag(jnp.where(off, a, 0.0))) for x, a in zip(t, a_list)]
        t = [x - _dot(y, _pair_diag(x)) for x, y in zip(t, m)]
    return t


def _gdn_kernel(prev_ref, main_ref, next_ref, small_ref, cw_ref, expa_ref, expb_ref, dtb_ref, a_ref,
                o_ref, ext_ref, s_ref, *, reverse):
    i = pl.program_id(0)
    n = pl.num_programs(0)
    blk = (n - 1 - i) if reverse else i
    T = CHUNK

    @pl.when(i == 0)
    def _():
        s_ref[...] = jnp.zeros_like(s_ref)

    qkv = _silu(_conv_halo(prev_ref, main_ref, next_ref, cw_ref, ext_ref, blk == 0, blk == n - 1, GDN_CONV))
    bdf = jnp.where(_block_diag_mask(), 1.0, 0.0)
    q = qkv[:, 0:GROUP_W]
    k = qkv[:, GROUP_W:2 * GROUP_W]
    v = qkv[:, 2 * GROUP_W:3 * GROUP_W]
    q = q * lax.rsqrt(_dot_split(q * q, bdf, 2) + EPS) * (HEAD_DIM ** -0.5)
    k = k * lax.rsqrt(_dot_split(k * k, bdf, 2) + EPS)
    small = small_ref[...]
    g = a_ref[...] * _softplus(_dot_split(small, expa_ref[...], 3) + dtb_ref[...])
    beta = _sigmoid(_dot_split(small, expb_ref[...], 3))
    ri, ci, vis, trif = _scan_masks(reverse)
    strict = vis & (ri != ci)
    r2 = lax.broadcasted_iota(I32, (T, 2 * T), 0)
    c2 = lax.broadcasted_iota(I32, (T, 2 * T), 1) % T
    eye2 = jnp.where(r2 == c2, 1.0, 0.0)
    bd16 = (r2 // 16) == (c2 // 16)
    offs = []
    for sz in (16, 32, 64):
        pair = (r2 // (2 * sz)) == (c2 // (2 * sz))
        later, earlier = (c2, r2) if reverse else (r2, c2)
        offs.append(pair & ((later // sz) % 2 == 1) & ((earlier // sz) % 2 == 0))
    kb = k * beta
    vb = v * beta
    hms = _head_masks(T)
    pre, a_pairs = [], []
    for s in range(SCAN_CHUNKS):
        r0, r1 = s * T, (s + 1) * T
        gc = _dot_split_rhs(trif, g[r0:r1], 3)
        g_tot = jnp.sum(g[r0:r1], axis=0, keepdims=True)
        egc = jnp.exp(gc)
        k_s, q_s = k[r0:r1], q[r0:r1]
        a_all = _dot_nt(_stack_heads(kb[r0:r1], hms), k_s)
        q_all = _dot_nt(_stack_heads(q_s, hms), k_s)
        a_h, qk_h = [], []
        for h in range(N_HEADS):
            col = jnp.broadcast_to(gc[:, h * HEAD_DIM:h * HEAD_DIM + 1], (T, T))
            decay = jnp.where(vis, jnp.exp(col - col.T), 0.0)
            a_h.append(jnp.where(strict, a_all[h * T:(h + 1) * T] * decay, 0.0))
            qk_h.append(q_all[h * T:(h + 1) * T] * decay)
        for pr in range(N_HEADS // 2):
            a_pairs.append(jnp.concatenate(a_h[2 * pr:2 * pr + 2], axis=1))
        rhs = jnp.concatenate([_stack_heads(vb[r0:r1], hms), _stack_heads(kb[r0:r1] * egc, hms)], axis=1)
        pre.append((jnp.concatenate(qk_h, axis=1), rhs, q_s * egc, k_s * jnp.exp(g_tot - gc), jnp.exp(g_tot)))
    t_pairs = _unit_tri_inverse_pairs(a_pairs, eye2, bd16, offs)
    npair = N_HEADS // 2
    uw = [_dot(jnp.concatenate(t_pairs[s * npair:(s + 1) * npair], axis=1), pre[s][1])
          for s in range(SCAN_CHUNKS)]
    state = s_ref[...]
    for s in (reversed(range(SCAN_CHUNKS)) if reverse else range(SCAN_CHUNKS)):
        qk_cat, _, q_dec, k_dec, decay = pre[s]
        v_new = uw[s][:, 0:GROUP_W] - _dot(uw[s][:, GROUP_W:2 * GROUP_W], state)
        o_ref[s * T:(s + 1) * T, :] = _dot(q_dec, state) + _dot(qk_cat, _stack_heads(v_new, hms))
        state = state * decay + bdf * _dot_tn(k_dec, v_new)
    s_ref[...] = state


def _gdn(cols_gdn, cols_small, conv_w, exp_a, exp_b, dt_bias, a_neg, reverse):
    L = cols_gdn.shape[0]
    rows = SCAN_CHUNKS * CHUNK
    nblk = L // rows
    prev, main, nxt, blk = _halo_specs(rows, XBC_W, nblk, reverse)
    const = lambda a: pl.BlockSpec(a.shape, lambda i: (0,) * a.ndim)
    return pl.pallas_call(
        functools.partial(_gdn_kernel, reverse=reverse),
        grid=(nblk,),
        in_specs=[prev, main, nxt, pl.BlockSpec((rows, 128), lambda i: (blk(i), 0)), const(conv_w),
                  const(exp_a), const(exp_b), const(dt_bias), const(a_neg)],
        out_specs=pl.BlockSpec((rows, GROUP_W), lambda i: (blk(i), 0)),
        out_shape=jax.ShapeDtypeStruct((L, GROUP_W), F32),
        scratch_shapes=[pltpu.VMEM((rows + 2 * HALO, XBC_W), F32),
                        pltpu.VMEM((GROUP_W, GROUP_W), F32)],
        compiler_params=_params(("arbitrary",)),
        name="gdn_bwd" if reverse else "gdn_fwd",
    )(cols_gdn, cols_gdn, cols_gdn, cols_small, conv_w, exp_a, exp_b, dt_bias, a_neg)


def _hy_pre_kernel(prev_ref, main_ref, next_ref, cw_ref, cb_ref, vx_ref, x0_ref, ext_ref):
    i = pl.program_id(0)
    n = pl.num_programs(0)
    u = _conv_halo(prev_ref, main_ref, next_ref, cw_ref, ext_ref, i == 0, i == n - 1, HY_CONV) + cb_ref[...]
    x0_ref[...] = u[:, 0:GROUP_W]
    vx_ref[...] = u[:, 2 * GROUP_W:3 * GROUP_W] * u[:, GROUP_W:2 * GROUP_W]


def _hy_pre(cols_hy, conv_w, conv_b, tm=512):
    L = cols_hy.shape[0]
    tm = min(tm, L)
    w = 3 * GROUP_W
    prev, main, nxt, _ = _halo_specs(tm, w, L // tm, False)
    const = lambda a: pl.BlockSpec(a.shape, lambda i: (0,) * a.ndim)
    return pl.pallas_call(
        _hy_pre_kernel,
        grid=(L // tm,),
        in_specs=[prev, main, nxt, const(conv_w), const(conv_b)],
        out_specs=[pl.BlockSpec((tm, GROUP_W), lambda i: (i, 0))] * 2,
        out_shape=[jax.ShapeDtypeStruct((L, GROUP_W), F32)] * 2,
        scratch_shapes=[pltpu.VMEM((tm + 2 * HALO, w), F32)],
        compiler_params=_params(("parallel",)),
        name="hyena_pre",
    )(cols_hy, cols_hy, cols_hy, conv_w, conv_b)


def _hy_filter_kernel(z_ref, win_ref, bin_ref, wmid_ref, bmid_ref, freq_ref, wout_ref, delta_ref,
                      f_ref, *, seq_len):
    i = pl.program_id(0)
    tm = z_ref.shape[0]
    half = tm // 2
    z = z_ref[...]
    zz = jnp.concatenate([z[0:half], z[half:tm]], axis=1)
    freq = freq_ref[...]
    h = jnp.sin(freq * (_dot_hi(zz, win_ref[...]) + bin_ref[...]))
    for j in range(HY_INNER):
        h = jnp.sin(freq * (_dot_hi(h, wmid_ref[j]) + bmid_ref[j]))
    row = i * tm + lax.broadcasted_iota(I32, (half, 1), 0)
    for part in range(2):
        out = _dot_hi(h[:, part * HY_ORDER:(part + 1) * HY_ORDER], wout_ref[...])
        t = z[part * half:(part + 1) * half, 0:1]
        out = out * (jnp.exp(-t * delta_ref[...]) + HY_SHIFT)
        f_ref[part * half:(part + 1) * half, :] = jnp.where(row + part * half == seq_len, 0.0, out)


def _hy_filter(z_ext, w_in2, b_in2, w_mid2, b_mid2, freq2, w_out, abs_delta, tm=512):
    n = z_ext.shape[0]
    L = n // 2
    tm = min(tm, L)
    nblk = n // tm
    const = lambda a: pl.BlockSpec(a.shape, lambda i: (0,) * a.ndim)
    return pl.pallas_call(
        functools.partial(_hy_filter_kernel, seq_len=L),
        grid=(nblk,),
        in_specs=[pl.BlockSpec((tm, HY_EMB_PAD), lambda i: (i, 0)), const(w_in2), const(b_in2),
                  const(w_mid2), const(b_mid2), const(freq2),
                  pl.BlockSpec((HY_ORDER, GROUP_W), lambda i: (0, i // (nblk // 2))), const(abs_delta)],
        out_specs=pl.BlockSpec((tm, GROUP_W), lambda i: (i, 0)),
        out_shape=jax.ShapeDtypeStruct((n, GROUP_W), F32),
        compiler_params=_params(("parallel",)),
        name="hyena_filter",
    )(z_ext, w_in2, b_in2, w_mid2, b_mid2, freq2, w_out, abs_delta)


def _left_matmul_kernel(m_ref, x_ref, o_ref):
    o_ref[...] = _dot(m_ref[...], x_ref[...])


def _left_matmul(m, x2d, cb=4096):
    rows, kdim = m.shape
    ncol = x2d.shape[1]
    cb = min(cb, ncol)
    return pl.pallas_call(
        _left_matmul_kernel,
        grid=(ncol // cb,),
        in_specs=[pl.BlockSpec((rows, kdim), lambda i: (0, 0)), pl.BlockSpec((kdim, cb), lambda i: (0, i))],
        out_specs=pl.BlockSpec((rows, cb), lambda i: (0, i)),
        out_shape=jax.ShapeDtypeStruct((rows, ncol), F32),
        compiler_params=_params(("parallel",)),
        name="hyena_outer_dft",
    )(m, x2d)


def _twiddle_inner_dft(a_ref, twc_ref, tws_ref, g_ref):
    ar = a_ref[0]
    ai = a_ref[1]
    cs = twc_ref[...]
    sn = tws_ref[...]
    br = ar * cs + ai * sn
    bi = ai * cs - ar * sn
    x = _dot(g_ref[...], jnp.concatenate([br, bi], axis=0))
    return x[0:FFT_N2], x[FFT_N2:2 * FFT_N2]


def _hy_filter_spec_kernel(a_ref, twc_ref, tws_ref, g_ref, x_ref):
    xr, xi = _twiddle_inner_dft(a_ref, twc_ref, tws_ref, g_ref)
    x_ref[0] = xr
    x_ref[1] = xi


def _hy_conv_spec_kernel(a_ref, twc_ref, tws_ref, g_ref, gi_ref, f_ref, e_ref):
    xr, xi = _twiddle_inner_dft(a_ref, twc_ref, tws_ref, g_ref)
    fr = f_ref[0]
    fi = f_ref[1]
    yr = xr * fr - xi * fi
    yi = xr * fi + xi * fr
    dd = _dot(gi_ref[...], jnp.concatenate([yr, yi], axis=0))
    dr = dd[0:FFT_N2]
    di = dd[FFT_N2:2 * FFT_N2]
    cs = twc_ref[...]
    sn = tws_ref[...]
    e_ref[0] = dr * cs - di * sn
    e_ref[1] = dr * sn + di * cs


def _hy_spec_specs():
    blk = pl.BlockSpec((2, None, FFT_N2, GROUP_W), lambda k: (0, k, 0, 0))
    tw = pl.BlockSpec((None, FFT_N2, 1), lambda k: (k, 0, 0))
    mat = pl.BlockSpec((2 * FFT_N2, 2 * FFT_N2), lambda k: (0, 0))
    return blk, tw, mat


def _hy_filter_spec(a4, twc, tws, g):
    blk, tw, mat = _hy_spec_specs()
    return pl.pallas_call(
        _hy_filter_spec_kernel,
        grid=(a4.shape[1],),
        in_specs=[blk, tw, tw, mat],
        out_specs=blk,
        out_shape=jax.ShapeDtypeStruct(a4.shape, F32),
        compiler_params=_params(("parallel",)),
        name="hyena_filter_spectrum",
    )(a4, twc, tws, g)


def _hy_conv_spec(a4, twc, tws, g, gi, fspec):
    blk, tw, mat = _hy_spec_specs()
    return pl.pallas_call(
        _hy_conv_spec_kernel,
        grid=(a4.shape[1],),
        in_specs=[blk, tw, tw, mat, mat, blk],
        out_specs=blk,
        out_shape=jax.ShapeDtypeStruct(a4.shape, F32),
        compiler_params=_params(("parallel",)),
        name="hyena_spectral_product",
    )(a4, twc, tws, g, gi, fspec)


def _postmix_kernel(h_ref, ret_ref, g_ref, ssdf_ref, ssdb_ref, zs_ref, hy_ref, vx_ref, x0_ref,
                    gdnf_ref, gdnb_ref, zg_ref, gnw_ref, snw_ref, hyb_ref, hnw_ref, dnw_ref,
                    wout_ref, fnw_ref, rw_ref, h1_ref, xn_ref, aff_ref):
    avg = jnp.where(_block_diag_mask(), 1.0 / HEAD_DIM, 0.0)
    o = ret_ref[0] + ret_ref[1]
    mu = _dot_split(o, avg, 2)
    var = _dot_split(jnp.square(o - mu), avg, 2)
    m_ret = _silu(g_ref[...]) * ((o - mu) * lax.rsqrt(var + EPS) * gnw_ref[...])

    y = (ssdf_ref[...] + ssdb_ref[...]) * _silu(zs_ref[...])
    m_ssm = y * lax.rsqrt(jnp.mean(y * y, axis=-1, keepdims=True) + EPS) * snw_ref[...]

    t = (hy_ref[...] + vx_ref[...] * hyb_ref[...]) * x0_ref[...]
    m_hy = t * lax.rsqrt(jnp.mean(t * t, axis=-1, keepdims=True) + EPS) * hnw_ref[...]

    o = gdnf_ref[...] + gdnb_ref[...]
    m_gdn = o * lax.rsqrt(_dot_split(o * o, avg, 2) + EPS) * dnw_ref[...] * _silu(zg_ref[...])

    h1 = h_ref[...]
    for gi, m in enumerate((m_ret, m_ssm, m_hy, m_gdn)):
        h1 = h1 + jnp.dot(m.astype(BF16), wout_ref[gi * GROUP_W:(gi + 1) * GROUP_W, :],
                          preferred_element_type=F32)
    h1_ref[...] = h1
    xn = h1 * lax.rsqrt(jnp.mean(h1 * h1, axis=-1, keepdims=True) + EPS) * fnw_ref[...]
    xn_ref[...] = xn.astype(BF16)
    logits = _dot_hi(xn, rw_ref[...])
    logits = logits - jnp.max(logits, axis=-1, keepdims=True)
    ex = jnp.exp(logits)
    aff_ref[...] = ex / jnp.sum(ex, axis=-1, keepdims=True)


def _postmix(h, ret_o, cols_ret, ssd_f, ssd_b, cols_ssm, hy_y, hy_vx, hy_x0, gdn_f, gdn_b, cols_gdn,
             gn_w, ssm_nw, hy_bias, hy_nw, gdn_nw, w_out, ffn_nw, router_w, tm=256):
    L = h.shape[0]
    tm = min(tm, L)
    row = lambda n: pl.BlockSpec((tm, n), lambda i: (i, 0))
    grp = row(GROUP_W)
    zcol = pl.BlockSpec((tm, GROUP_W), lambda i: (i, 3))
    pair = pl.BlockSpec((2, tm, GROUP_W), lambda i: (0, i, 0))
    const = lambda a: pl.BlockSpec(a.shape, lambda i: (0,) * a.ndim)
    return pl.pallas_call(
        _postmix_kernel,
        grid=(L // tm,),
        in_specs=[row(D_MODEL), pair, zcol, grp, grp, zcol, grp, grp, grp, grp, grp, zcol,
                  const(gn_w), const(ssm_nw), const(hy_bias), const(hy_nw), const(gdn_nw),
                  const(w_out), const(ffn_nw), const(router_w)],
        out_specs=[row(D_MODEL), row(D_MODEL), row(N_EXPERTS)],
        out_shape=[jax.ShapeDtypeStruct((L, D_MODEL), F32), jax.ShapeDtypeStruct((L, D_MODEL), BF16),
                   jax.ShapeDtypeStruct((L, N_EXPERTS), F32)],
        compiler_params=_params(("parallel",)),
        name="postmix",
    )(h, ret_o, cols_ret, ssd_f, ssd_b, cols_ssm, hy_y, hy_vx, hy_x0, gdn_f, gdn_b, cols_gdn,
      gn_w, ssm_nw, hy_bias, hy_nw, gdn_nw, w_out, ffn_nw, router_w)


def _topk_kernel(aff_ref, sel_ref, *, cap, idx_bits):
    bits = lax.bitcast_convert_type(aff_ref[...], I32)
    shape = bits.shape
    capf = float(cap)

    def count(mask):
        return jnp.sum(jnp.where(mask, 1.0, 0.0), axis=1, keepdims=True)

    def value_step(i, thr):
        cand = thr | lax.shift_left(jnp.int32(1), 30 - i)
        return jnp.where(count(bits >= cand) >= capf, cand, thr)

    thr = lax.fori_loop(0, 31, value_step, jnp.zeros((shape[0], 1), I32))
    above = bits > thr
    tied = bits == thr
    need = capf - count(above)
    idx = lax.broadcasted_iota(I32, shape, 1)

    def index_step(i, m):
        cand = m | lax.shift_left(jnp.int32(1), idx_bits - 1 - i)
        return jnp.where(count(tied & (idx < cand)) < need, cand, m)

    last = lax.fori_loop(0, idx_bits, index_step, jnp.zeros((shape[0], 1), I32))
    sel_ref[...] = jnp.where(above | (tied & (idx <= last)), 1.0, 0.0)


def _topk(aff_t, cap):
    n_e, L = aff_t.shape
    return pl.pallas_call(
        functools.partial(_topk_kernel, cap=cap, idx_bits=int(math.log2(L)) + 1),
        out_shape=jax.ShapeDtypeStruct((n_e, L), F32),
        compiler_params=_params(None),
        name="expert_topk",
    )(aff_t)


def _positions_kernel(sel_ref, posm_ref, offs_ref, *, n_e):
    s = sel_ref[...]
    nb = s.shape[0] // n_e
    r = lax.broadcasted_iota(I32, (128, 128), 0)
    c = lax.broadcasted_iota(I32, (128, 128), 1)
    incl = _dot(s, jnp.where(r <= c, 1.0, 0.0))
    tot = _dot(s, jnp.ones((128, 128), F32))
    rb = lax.broadcasted_iota(I32, (nb, nb), 0)
    cbk = lax.broadcasted_iota(I32, (nb, nb), 1)
    before = jnp.where(cbk < rb, 1.0, 0.0)
    for e in range(n_e):
        offs = _dot(before, tot[e * nb:(e + 1) * nb])
        se = s[e * nb:(e + 1) * nb]
        pos = incl[e * nb:(e + 1) * nb] - se + offs
        posm_ref[e * nb:(e + 1) * nb, :] = jnp.where(se > 0.5, pos, -1.0).astype(I32)
        offs_ref[e * nb:(e + 1) * nb, :] = offs.astype(I32)


def _positions(sel2, n_e):
    return pl.pallas_call(
        functools.partial(_positions_kernel, n_e=n_e),
        out_shape=[jax.ShapeDtypeStruct(sel2.shape, I32)] * 2,
        compiler_params=_params(None),
        name="expert_positions",
    )(sel2)


def _tile_hits(posm_row, tile_index):
    rows = tile_index * ROW_TILE + lax.broadcasted_iota(I32, (ROW_TILE, posm_row.shape[1]), 0)
    return posm_row == rows


def _gather_kernel(off_ref, j_ref, b_ref, posm_ref, aff_ref, x_ref, o_ref, og_ref):
    e = pl.program_id(0)
    tb = posm_ref.shape[2]
    o_ref[...] = jnp.zeros_like(o_ref)
    og_ref[...] = jnp.zeros_like(og_ref)

    def item(i, carry):
        j = j_ref[i]
        b = b_ref[i]
        hit = _tile_hits(posm_ref[b], j)
        rows = pl.ds(pl.multiple_of(j * ROW_TILE, ROW_TILE), ROW_TILE)
        x_blk = x_ref[pl.ds(pl.multiple_of(b * tb, tb), tb), :]
        o_ref[rows, :] += jnp.dot(jnp.where(hit, 1.0, 0.0).astype(BF16), x_blk,
                                  preferred_element_type=F32).astype(BF16)
        og_ref[rows, :] += jnp.sum(jnp.where(hit, aff_ref[b], 0.0), axis=1, keepdims=True)
        return carry

    lax.fori_loop(off_ref[e], off_ref[e + 1], item, 0)


def _gather(work, posm4, aff4, xn, cap):
    n_e, nblk, _, tb = posm4.shape
    per_expert = pl.BlockSpec((None, nblk, 1, tb), lambda e, *_: (e, 0, 0, 0))
    return pl.pallas_call(
        _gather_kernel,
        grid_spec=pltpu.PrefetchScalarGridSpec(
            num_scalar_prefetch=3,
            grid=(n_e,),
            in_specs=[per_expert, per_expert, pl.BlockSpec(memory_space=pltpu.VMEM)],
            out_specs=[pl.BlockSpec((None, cap, D_MODEL), lambda e, *_: (e, 0, 0)),
                       pl.BlockSpec((None, cap, 1), lambda e, *_: (e, 0, 0))],
        ),
        out_shape=[jax.ShapeDtypeStruct((n_e, cap, D_MODEL), BF16),
                   jax.ShapeDtypeStruct((n_e, cap, 1), F32)],
        compiler_params=_params(("arbitrary",), VMEM_LIMIT_RESIDENT),
        name="expert_gather",
    )(*work, posm4, aff4, xn)


def _ffn_kernel(x_ref, gate_ref, wg_ref, wu_ref, wd_ref, o_ref, acc_ref):
    f = pl.program_id(1)

    @pl.when(f == 0)
    def _():
        acc_ref[...] = jnp.zeros_like(acc_ref)

    x = x_ref[...]
    gate = jnp.dot(x, wg_ref[...].astype(BF16), preferred_element_type=F32)
    up = jnp.dot(x, wu_ref[...].astype(BF16), preferred_element_type=F32)
    acc_ref[...] += _dot(_silu(gate) * up, wd_ref[...])

    @pl.when(f == pl.num_programs(1) - 1)
    def _():
        o_ref[...] = (acc_ref[...] * gate_ref[...]).astype(BF16)


def _expert_ffn(xs, gate_c, w_gate, w_up, w_down, layer, group, fb=256):
    n_e, cap, _ = xs.shape
    n_g = n_e // EXPERT_GROUPS
    e0 = group * n_g
    return pl.pallas_call(
        _ffn_kernel,
        grid=(n_g, EXPERT_FF // fb),
        in_specs=[pl.BlockSpec((None, cap, D_MODEL), lambda e, f: (e0 + e, 0, 0)),
                  pl.BlockSpec((None, cap, 1), lambda e, f: (e0 + e, 0, 0)),
                  pl.BlockSpec((None, None, D_MODEL, fb), lambda e, f: (layer, e0 + e, 0, f)),
                  pl.BlockSpec((None, None, D_MODEL, fb), lambda e, f: (layer, e0 + e, 0, f)),
                  pl.BlockSpec((None, None, fb, D_MODEL), lambda e, f: (layer, e0 + e, f, 0))],
        out_specs=pl.BlockSpec((None, cap, D_MODEL), lambda e, f: (e, 0, 0)),
        out_shape=jax.ShapeDtypeStruct((n_g, cap, D_MODEL), BF16),
        scratch_shapes=[pltpu.VMEM((cap, D_MODEL), F32)],
        compiler_params=_params(("parallel", "arbitrary")),
        name="expert_ffn",
    )(xs, gate_c, w_gate, w_up, w_down)


def _scatter_kernel(off_ref, e_ref, j_ref, posm_ref, y_ref, o_ref):
    b = pl.program_id(0)
    o_ref[...] = jnp.zeros_like(o_ref)

    def item(i, carry):
        e = e_ref[i]
        j = j_ref[i]
        p = jnp.where(_tile_hits(posm_ref[e], j), 1.0, 0.0).astype(BF16)
        y = y_ref[e, pl.ds(pl.multiple_of(j * ROW_TILE, ROW_TILE), ROW_TILE), :]
        o_ref[...] += lax.dot_general(p, y, (((0,), (0,)), ((), ())), preferred_element_type=F32)
        return carry

    lax.fori_loop(off_ref[b], off_ref[b + 1], item, 0)


def _scatter(work, posm4, ye, group):
    n_e, nblk, _, tb = posm4.shape
    n_g = ye.shape[0]
    return pl.pallas_call(
        _scatter_kernel,
        grid_spec=pltpu.PrefetchScalarGridSpec(
            num_scalar_prefetch=3,
            grid=(nblk,),
            in_specs=[pl.BlockSpec((n_g, None, 1, tb), lambda b, *_: (group, b, 0, 0)),
                      pl.BlockSpec(memory_space=pltpu.VMEM)],
            out_specs=pl.BlockSpec((tb, D_MODEL), lambda b, *_: (b, 0)),
        ),
        out_shape=jax.ShapeDtypeStruct((nblk * tb, D_MODEL), F32),
        compiler_params=_params(("arbitrary",), VMEM_LIMIT_RESIDENT),
        name="expert_scatter",
    )(*work, posm4, ye)


def _enumerate_items(n_items, first_tile, n_work):
    n_groups, n_members = n_items.shape
    flat = n_items.reshape(-1)
    ends = jnp.cumsum(flat)
    begins = ends - flat
    w = jnp.arange(n_work, dtype=I32)
    pair = jnp.sum((ends[None, :] <= w[:, None]).astype(I32), axis=1)
    onehot = pair[:, None] == jnp.arange(flat.shape[0], dtype=I32)[None, :]
    begin_w = jnp.sum(jnp.where(onehot, begins[None, :], 0), axis=1)
    first_w = jnp.sum(jnp.where(onehot, first_tile.reshape(-1)[None, :], 0), axis=1)
    offsets = jnp.concatenate([jnp.zeros((1,), I32), ends.reshape(n_groups, n_members)[:, -1]])
    return offsets.astype(I32), (pair % n_members).astype(I32), (first_w + w - begin_w).astype(I32)


def _work_lists(offs, cap, tb):
    n_e, nb128 = offs.shape
    per = tb // 128
    nblk = nb128 // per
    ntile = cap // ROW_TILE
    start = offs[:, ::per]
    end = jnp.concatenate([start[:, 1:], jnp.full((n_e, 1), cap, I32)], axis=1)
    first_tile = jnp.minimum(start // ROW_TILE, ntile - 1)
    n_items = jnp.where(end > start, (end - 1) // ROW_TILE - first_tile + 1, 0)
    n_work = n_e * (ntile + nblk - 1)
    off_g, b_g, j_g = _enumerate_items(n_items, first_tile, n_work)
    n_g = n_e // EXPERT_GROUPS
    scatter_work = []
    for grp in range(EXPERT_GROUPS):
        sl = slice(grp * n_g, (grp + 1) * n_g)
        off_s, e_s, j_s = _enumerate_items(n_items[sl].T, first_tile[sl].T, n_g * (ntile + nblk - 1))
        scatter_work.append((off_s, e_s, j_s))
    return (off_g, j_g, b_g), scatter_work


def _ple_kernel(h_ref, moe_a_ref, moe_b_ref, p_ref, nw_ref, wg_ref, wp_ref, fw_ref, o_ref, *, final):
    h2 = h_ref[...] + (moe_a_ref[...] + moe_b_ref[...])
    hn = h2 * lax.rsqrt(jnp.mean(h2 * h2, axis=-1, keepdims=True) + EPS) * nw_ref[...]
    gate = _sigmoid(jnp.dot(hn.astype(BF16), wg_ref[...], preferred_element_type=F32))
    h3 = h2 + gate * jnp.dot(p_ref[...].astype(BF16), wp_ref[...], preferred_element_type=F32)
    if final:
        h3 = h3 * lax.rsqrt(jnp.mean(h3 * h3, axis=-1, keepdims=True) + EPS) * fw_ref[...]
    o_ref[...] = h3


def _ple(h1, moe_a, moe_b, p, layer, norm_w, w_gate, w_proj, final_w, final, tm=512):
    L = h1.shape[0]
    tm = min(tm, L)
    row = lambda n: pl.BlockSpec((tm, n), lambda i: (i, 0))
    const = lambda a: pl.BlockSpec(a.shape, lambda i: (0,) * a.ndim)
    return pl.pallas_call(
        functools.partial(_ple_kernel, final=final),
        grid=(L // tm,),
        in_specs=[row(D_MODEL), row(D_MODEL), row(D_MODEL),
                  pl.BlockSpec((None, None, tm, PLE_DIM), lambda i: (layer, 0, i, 0)),
                  const(norm_w), const(w_gate), const(w_proj), const(final_w)],
        out_specs=row(D_MODEL),
        out_shape=jax.ShapeDtypeStruct((L, D_MODEL), F32),
        compiler_params=_params(("parallel",)),
        name="ple",
    )(h1, moe_a, moe_b, p, norm_w, w_gate, w_proj, final_w)


def _expand_table(base):
    t = np.zeros((2, 128, GROUP_W), np.float32)
    for d in range(2):
        for h in range(N_HEADS):
            t[d, base + N_HEADS * d + h, h * HEAD_DIM:(h + 1) * HEAD_DIM] = 1.0
    return t


def _retention_tables():
    hh = np.arange(N_HEADS, dtype=np.float64)
    lg = np.log(1.0 - 2.0 ** (-5.0 - hh))
    t = np.arange(CHUNK, dtype=np.float64)
    dmask = np.exp(np.abs(t[:, None] - t[None, :])[None] * lg[:, None, None])
    lanes = np.repeat(lg, HEAD_DIM)[None, :]
    pwt = np.exp((CHUNK - t)[:, None] * lanes)
    pwh = np.exp(t[:, None] * lanes)
    dec = np.exp(CHUNK * lanes)
    return [a.astype(np.float32) for a in (dmask, pwh, pwt, dec)]


def _dft_tables(n1):
    n = n1 * FFT_N2
    k1 = np.arange(n1, dtype=np.float64)
    ang1 = 2.0 * np.pi * np.outer(k1, k1) / n1
    c1, s1 = np.cos(ang1), np.sin(ang1)
    f1 = np.concatenate([c1, -s1], axis=0)
    f3 = np.concatenate([c1[:n1 // 2], -s1[:n1 // 2]], axis=1) / n
    k2 = np.arange(FFT_N2, dtype=np.float64)
    ang2 = 2.0 * np.pi * np.outer(k2, k2) / FFT_N2
    c2, s2 = np.cos(ang2), np.sin(ang2)
    g = np.block([[c2, s2], [-s2, c2]])
    gi = np.block([[c2, -s2], [s2, c2]])
    angt = 2.0 * np.pi * np.outer(k1, k2) / n
    twc, tws = np.cos(angt)[:, :, None], np.sin(angt)[:, :, None]
    return [a.astype(np.float32) for a in (f1, f3, g, gi, twc, tws)]


def _hyena_positions(L):
    n = jnp.arange(2 * L, dtype=I32)
    m = jnp.where(n <= L, n, 2 * L - n).astype(F32)[:, None]
    t = m / (L - 1)
    bands = jnp.linspace(1e-4, HY_BANDS - 1, HY_BANDS, dtype=F32)
    ang = (2.0 * math.pi / L) * m * bands[None]
    return jnp.concatenate([t, jnp.cos(ang), -jnp.sin(ang), jnp.zeros((2 * L, HY_EMB_PAD - HY_EMB), F32)], axis=-1)


def _rope_tables(L):
    inv = ROPE_BASE ** (-jnp.arange(0, HEAD_DIM, 2, dtype=F32) / HEAD_DIM)
    ang = jnp.arange(L, dtype=F32)[:, None] * inv[None]
    cos, sin = jnp.cos(ang), jnp.sin(ang)
    cos_t = jnp.tile(jnp.concatenate([cos, cos], axis=-1), (1, N_HEADS))
    sin_t = jnp.tile(jnp.concatenate([-sin, sin], axis=-1), (1, N_HEADS))
    return cos_t, sin_t


def _lanes(v):
    return jnp.repeat(v, HEAD_DIM, axis=-1)[..., None, :]


def _twice_diag(w):
    z = jnp.zeros_like(w)
    return jnp.concatenate([jnp.concatenate([w, z], axis=-1), jnp.concatenate([z, w], axis=-1)], axis=-2)


def kernel(x, p, norm_mix_w, w_in, ret_gn_w, ssm_conv_w, ssm_conv_b, ssm_a_log, ssm_dt_bias, ssm_d, ssm_norm_w, hy_conv_w, hy_conv_b, hy_filt_w_in, hy_filt_b_in, hy_filt_w_mid, hy_filt_b_mid, hy_filt_freq, hy_filt_w_out, hy_bias, hy_norm_w, gdn_conv_w, gdn_a_log, gdn_dt_bias, gdn_norm_w, w_out, norm_ffn_w, router_w, exp_w_gate, exp_w_up, exp_w_down, ple_norm_w, ple_gate_w, ple_proj_w, final_norm_w):
    batch, L, _ = x.shape
    depth = w_in.shape[0]
    assert batch == 1 and L % TOKEN_BLOCK == 0 and (2 * L) % (8 * FFT_N2) == 0
    cap = CAPACITY_FACTOR * L // N_EXPERTS
    assert cap % ROW_TILE == 0
    n1 = 2 * L // FFT_N2
    nblk = L // TOKEN_BLOCK

    dmask, pwh, pwt, dec = (jnp.asarray(a) for a in _retention_tables())
    f1, f3, g_mat, gi_mat, twc, tws = (jnp.asarray(a) for a in _dft_tables(n1))
    exp_dt, exp_a, exp_b = (jnp.asarray(_expand_table(b)) for b in (0, 8, 16))
    cos_t, sin_t = _rope_tables(L)
    z_ext = _hyena_positions(L)
    max_decay = math.log(HY_DECAY_TARGET) / HY_FAST_PCT
    min_decay = math.log(HY_DECAY_TARGET) / HY_SLOW_PCT
    abs_delta = jnp.abs(jnp.linspace(min_decay, max_decay, GROUP_W, dtype=F32))[None, :]
    row = lambda v: v[None, :]
    twice = lambda v: jnp.concatenate([v, v], axis=-1)

    h = x[0]
    for i in range(depth):
        w = w_in[i]
        o_ssm = 4 * GROUP_W
        o_hy = o_ssm + GROUP_W + XBC_W + 2 * N_HEADS
        o_gdn = o_hy + 3 * GROUP_W
        o_ab = o_gdn + 4 * GROUP_W
        w_ret = w[:, 0:o_ssm].astype(BF16)
        w_ssm = jnp.concatenate([w[:, o_ssm + GROUP_W:o_ssm + GROUP_W + XBC_W], w[:, o_ssm:o_ssm + GROUP_W]],
                                axis=1).astype(BF16)
        w_hy = w[:, o_hy:o_gdn].astype(BF16)
        w_gdn = w[:, o_gdn:o_ab].astype(BF16)
        w_small = jnp.concatenate([w[:, o_hy - 2 * N_HEADS:o_hy], w[:, o_ab:o_ab + 4 * N_HEADS],
                                   jnp.zeros((D_MODEL, 128 - 6 * N_HEADS), F32)], axis=1).astype(BF16)
        c_ret, c_ssm, c_hy, c_gdn, c_small = _inproj(h, row(norm_mix_w[i]), w_ret, w_ssm, w_hy, w_gdn, w_small)

        ret_o = _retention(c_ret, cos_t, sin_t, dmask, pwh, pwt, dec)

        ssm_dtb = _lanes(ssm_dt_bias[i])
        ssm_a = _lanes(-jnp.exp(ssm_a_log[i]))
        ssd_f, ssd_b = (_ssd(c_ssm, c_small, ssm_conv_w[i], row(ssm_conv_b[i]), exp_dt[d], ssm_dtb[d], ssm_a[d],
                             _lanes(ssm_d[i]), reverse=bool(d)) for d in range(2))

        gdn_dtb = _lanes(gdn_dt_bias[i])
        gdn_a = _lanes(-jnp.exp(gdn_a_log[i]))
        gdn_f, gdn_b = (_gdn(c_gdn, c_small, gdn_conv_w[i], exp_a[d], exp_b[d], gdn_dtb[d], gdn_a[d],
                             reverse=bool(d)) for d in range(2))

        hy_vx, hy_x0 = _hy_pre(c_hy, hy_conv_w[i], row(hy_conv_b[i]))
        w_in_pad = jnp.concatenate([hy_filt_w_in[i], jnp.zeros((HY_EMB_PAD - HY_EMB, HY_ORDER), F32)], axis=0)
        filt = _hy_filter(z_ext, _twice_diag(w_in_pad), row(twice(hy_filt_b_in[i])), _twice_diag(hy_filt_w_mid[i]),
                          twice(hy_filt_b_mid[i])[:, None, :], row(twice(hy_filt_freq[i])), hy_filt_w_out[i],
                          abs_delta)
        spec_shape = (2, n1, FFT_N2, GROUP_W)
        f_a = _left_matmul(f1, filt.reshape(n1, FFT_N2 * GROUP_W)).reshape(spec_shape)
        f_spec = _hy_filter_spec(f_a, twc, tws, g_mat)
        v_a = _left_matmul(f1[:, :n1 // 2], hy_vx.reshape(n1 // 2, FFT_N2 * GROUP_W)).reshape(spec_shape)
        e_spec = _hy_conv_spec(v_a, twc, tws, g_mat, gi_mat, f_spec)
        hy_y = _left_matmul(f3, e_spec.reshape(2 * n1, FFT_N2 * GROUP_W)).reshape(L, GROUP_W)

        h1, xn, aff = _postmix(h, ret_o, c_ret, ssd_f, ssd_b, c_ssm, hy_y, hy_vx, hy_x0, gdn_f, gdn_b, c_gdn,
                               row(ret_gn_w[i]), row(ssm_norm_w[i]), row(hy_bias[i]), row(hy_norm_w[i]),
                               row(jnp.tile(gdn_norm_w[i], N_HEADS)), w_out[i].astype(BF16),
                               row(norm_ffn_w[i]), router_w[i])

        aff_t = aff.T
        sel = _topk(aff_t, cap)
        posm, offs = _positions(sel.reshape(N_EXPERTS * (L // 128), 128), N_EXPERTS)
        posm4 = posm.reshape(N_EXPERTS, nblk, 1, TOKEN_BLOCK)
        aff4 = aff_t.reshape(N_EXPERTS, nblk, 1, TOKEN_BLOCK)
        gather_work, scatter_work = _work_lists(offs[:, 0].reshape(N_EXPERTS, L // 128), cap, TOKEN_BLOCK)
        xs, gate_c = _gather(gather_work, posm4, aff4, xn, cap)
        moe = [_scatter(scatter_work[grp], posm4,
                        _expert_ffn(xs, gate_c, exp_w_gate, exp_w_up, exp_w_down, i, grp), grp)
               for grp in range(EXPERT_GROUPS)]

        h = _ple(h1, moe[0], moe[1], p, i, row(ple_norm_w[i]), ple_gate_w[i].astype(BF16), ple_proj_w[i].astype(BF16),
                 row(final_norm_w), final=(i == depth - 1))
    return h[None]
```

```python
import functools
import math

import numpy as np
import jax
import jax.numpy as jnp
from jax import lax
from jax.experimental import pallas as pl
from jax.experimental.pallas import tpu as pltpu

F32 = jnp.float32
BF16 = jnp.bfloat16
I32 = jnp.int32
HIGHEST = lax.Precision.HIGHEST

D_MODEL = 1024
GROUP_W = 256
HEAD_DIM = 64
N_HEADS = GROUP_W // HEAD_DIM
CHUNK = 128
SCAN_CHUNKS = 4
RET_CHUNKS = 4
EPS = 1e-6
ROPE_BASE = 10000.0
SSM_STATE = 128
SSM_CONV = 5
XBC_W = 3 * GROUP_W
HY_CONV = 3
HY_EMB = 33
HY_EMB_PAD = 64
HY_BANDS = (HY_EMB - 1) // 2
HY_ORDER = 64
HY_INNER = 2
HY_DECAY_TARGET = 1e-2
HY_FAST_PCT = 0.3
HY_SLOW_PCT = 1.5
HY_SHIFT = 0.05
GDN_CONV = 5
N_EXPERTS = 16
CAPACITY_FACTOR = 2
EXPERT_FF = 1024
PLE_DIM = 256

HALO = 8
FFT_N2 = 256
FFT_K1_STEP = 8
ROW_TILE = 128
ROW_ALIGN = 16
TOKEN_BLOCK = 512
EXPERT_GROUPS = 2
VMEM_LIMIT = 48 * 1024 * 1024
VMEM_LIMIT_RESIDENT = 58 * 1024 * 1024


def _dot(a, b):
    return jnp.dot(a.astype(BF16), b.astype(BF16), preferred_element_type=F32)


def _dot_hi(a, b):
    return jnp.dot(a, b, precision=HIGHEST, preferred_element_type=F32)


def _dot_3pass(a, b):
    ah = a.astype(BF16)
    bh = b.astype(BF16)
    al = (a - ah.astype(F32)).astype(BF16)
    bl = (b - bh.astype(F32)).astype(BF16)
    dot = functools.partial(jnp.dot, preferred_element_type=F32)
    return dot(ah, bh) + (dot(ah, bl) + dot(al, bh))


def _dot_split(a, b, terms):
    bb = b.astype(BF16)
    out, rest = None, a
    for _ in range(terms):
        piece = rest.astype(BF16)
        part = jnp.dot(piece, bb, preferred_element_type=F32)
        out = part if out is None else out + part
        rest = rest - piece.astype(F32)
    return out


def _dot_split_rhs(a, b, terms):
    ab = a.astype(BF16)
    out, rest = None, b
    for _ in range(terms):
        piece = rest.astype(BF16)
        part = jnp.dot(ab, piece, preferred_element_type=F32)
        out = part if out is None else out + part
        rest = rest - piece.astype(F32)
    return out


def _dot_nt(a, b):
    return lax.dot_general(a.astype(BF16), b.astype(BF16), (((1,), (1,)), ((), ())),
                           preferred_element_type=F32)


def _dot_tn(a, b):
    return lax.dot_general(a.astype(BF16), b.astype(BF16), (((0,), (0,)), ((), ())),
                           preferred_element_type=F32)


def _silu(x):
    return x * (1.0 / (1.0 + jnp.exp(-x)))


def _sigmoid(x):
    return 1.0 / (1.0 + jnp.exp(-x))


def _softplus(x):
    return jnp.maximum(x, 0.0) + jnp.log(1.0 + jnp.exp(-jnp.abs(x)))


def _head_masks(rows):
    lane = lax.broadcasted_iota(I32, (rows, GROUP_W), 1)
    return [(lane >= h * HEAD_DIM) & (lane < (h + 1) * HEAD_DIM) for h in range(N_HEADS)]


def _stack_heads(x, hms):
    return jnp.concatenate([jnp.where(hm, x, 0.0) for hm in hms], axis=0)


def _pair_diag(x):
    t = x.shape[0]
    left = lax.broadcasted_iota(I32, x.shape, 1) < t
    return jnp.concatenate([jnp.where(left, x, 0.0), jnp.where(left, 0.0, x)], axis=0)


def _block_diag_mask():
    r = lax.broadcasted_iota(I32, (GROUP_W, GROUP_W), 0) // HEAD_DIM
    c = lax.broadcasted_iota(I32, (GROUP_W, GROUP_W), 1) // HEAD_DIM
    return r == c


def _params(sem, vmem=VMEM_LIMIT):
    return pltpu.CompilerParams(dimension_semantics=sem, vmem_limit_bytes=vmem)


def _conv_halo(prev_ref, main_ref, next_ref, w_ref, ext_ref, is_first, is_last, width):
    rows = main_ref.shape[0]
    ext_ref[0:HALO, :] = jnp.where(is_first, 0.0, prev_ref[...])
    ext_ref[HALO:HALO + rows, :] = main_ref[...]
    ext_ref[HALO + rows:2 * HALO + rows, :] = jnp.where(is_last, 0.0, next_ref[...])
    pad = (width - 1) // 2
    acc = None
    for k in range(width):
        term = ext_ref[pl.ds(HALO - pad + k, rows), :] * w_ref[k:k + 1, :]
        acc = term if acc is None else acc + term
    return acc


def _halo_specs(rows, width, nblocks, reverse):
    per = rows // HALO
    last8 = nblocks * per - 1
    blk = (lambda i: nblocks - 1 - i) if reverse else (lambda i: i)
    prev = pl.BlockSpec((HALO, width), lambda i: (jnp.maximum(blk(i) * per - 1, 0), 0))
    main = pl.BlockSpec((rows, width), lambda i: (blk(i), 0))
    nxt = pl.BlockSpec((HALO, width), lambda i: (jnp.minimum((blk(i) + 1) * per, last8), 0))
    return prev, main, nxt, blk


def _scan_masks(reverse):
    ri = lax.broadcasted_iota(I32, (CHUNK, CHUNK), 0)
    ci = lax.broadcasted_iota(I32, (CHUNK, CHUNK), 1)
    vis = (ci >= ri) if reverse else (ci <= ri)
    return ri, ci, vis, jnp.where(vis, 1.0, 0.0)


def _inproj_kernel(h_ref, nw_ref, wr_ref, ws_ref, wh_ref, wg_ref, wm_ref,
                   o_ret, o_ssm, o_hy, o_gdn, o_small):
    x = h_ref[...]
    y = x * lax.rsqrt(jnp.mean(x * x, axis=-1, keepdims=True) + EPS) * nw_ref[...]
    yb = y.astype(BF16)
    o_ret[...] = jnp.dot(yb, wr_ref[...], preferred_element_type=F32)
    o_ssm[...] = jnp.dot(yb, ws_ref[...], preferred_element_type=F32)
    o_hy[...] = jnp.dot(yb, wh_ref[...], preferred_element_type=F32)
    o_gdn[...] = jnp.dot(yb, wg_ref[...], preferred_element_type=F32)
    o_small[...] = jnp.dot(yb, wm_ref[...], preferred_element_type=F32)


def _inproj(h, norm_w, w_ret, w_ssm, w_hy, w_gdn, w_small, tm=512):
    L = h.shape[0]
    tm = min(tm, L)
    widths = [w.shape[1] for w in (w_ret, w_ssm, w_hy, w_gdn, w_small)]
    full = lambda w: pl.BlockSpec(w.shape, lambda i: (0, 0))
    return pl.pallas_call(
        _inproj_kernel,
        grid=(L // tm,),
        in_specs=[pl.BlockSpec((tm, D_MODEL), lambda i: (i, 0)), full(norm_w),
                  full(w_ret), full(w_ssm), full(w_hy), full(w_gdn), full(w_small)],
        out_specs=[pl.BlockSpec((tm, n), lambda i: (i, 0)) for n in widths],
        out_shape=[jax.ShapeDtypeStruct((L, n), F32) for n in widths],
        compiler_params=_params(("parallel",)),
        name="inproj",
    )(h, norm_w, w_ret, w_ssm, w_hy, w_gdn, w_small)


def _rope(x, cos, sin_signed, first_half):
    half = HEAD_DIM // 2
    swapped = jnp.where(first_half, pltpu.roll(x, GROUP_W - half, 1), pltpu.roll(x, half, 1))
    return x * cos + swapped * sin_signed


def _ret_kernel(cols_ref, cos_ref, sin_ref, dmask_ref, pwh_ref, pwt_ref, dec_ref, o_ref, s_ref, *, reverse):
    i = pl.program_id(0)
    T = CHUNK

    @pl.when(i == 0)
    def _():
        s_ref[...] = jnp.zeros_like(s_ref)

    rows = cols_ref.shape[0]
    lane = lax.broadcasted_iota(I32, (rows, GROUP_W), 1)
    first_half = (lane % HEAD_DIM) < (HEAD_DIM // 2)
    cos = cos_ref[...]
    sin = sin_ref[...]
    q = _rope(cols_ref[:, 0:GROUP_W], cos, sin, first_half)
    k = _rope(cols_ref[:, GROUP_W:2 * GROUP_W], cos, sin, first_half) * (HEAD_DIM ** -0.5)
    v = cols_ref[:, 2 * GROUP_W:3 * GROUP_W]
    pw_q = pwt_ref[...] if reverse else pwh_ref[...]
    pw_k = pwh_ref[...] if reverse else pwt_ref[...]
    bd = _block_diag_mask()
    hms = _head_masks(T)
    n_sub = rows // T
    intra = []
    if not reverse:
        for s in range(n_sub):
            q_s, k_s, v_s = (a[s * T:(s + 1) * T] for a in (q, k, v))
            s_all = _dot_nt(_stack_heads(q_s, hms), k_s)
            s_cat = jnp.concatenate([s_all[h * T:(h + 1) * T] * dmask_ref[h] for h in range(N_HEADS)], axis=1)
            intra.append(_dot(s_cat, _stack_heads(v_s, hms)))
    state = s_ref[...]
    for s in (reversed(range(n_sub)) if reverse else range(n_sub)):
        q_s, k_s, v_s = (a[s * T:(s + 1) * T] for a in (q, k, v))
        o = _dot(q_s * pw_q, state)
        o_ref[s * T:(s + 1) * T, :] = o if reverse else o + intra[s]
        state = dec_ref[...] * state + jnp.where(bd, _dot_tn(k_s * pw_k, v_s), 0.0)
    s_ref[...] = state


def _retention(cols_ret, cos_t, sin_t, dmask, pwh, pwt, dec, reverse):
    L = cols_ret.shape[0]
    rows = RET_CHUNKS * CHUNK
    nblk = L // rows
    blk = (lambda i: (nblk - 1 - i, 0)) if reverse else (lambda i: (i, 0))
    const = lambda a: pl.BlockSpec(a.shape, lambda i: (0,) * a.ndim)
    return pl.pallas_call(
        functools.partial(_ret_kernel, reverse=reverse),
        grid=(nblk,),
        in_specs=[pl.BlockSpec((rows, 4 * GROUP_W), blk),
                  pl.BlockSpec((rows, GROUP_W), blk), pl.BlockSpec((rows, GROUP_W), blk),
                  const(dmask), const(pwh), const(pwt), const(dec)],
        out_specs=pl.BlockSpec((rows, GROUP_W), blk),
        out_shape=jax.ShapeDtypeStruct((L, GROUP_W), F32),
        scratch_shapes=[pltpu.VMEM((GROUP_W, GROUP_W), F32)],
        compiler_params=_params(("arbitrary",)),
        name="retention_bwd" if reverse else "retention_fwd",
    )(cols_ret, cos_t, sin_t, dmask, pwh, pwt, dec)


def _ssd_kernel(prev_ref, main_ref, next_ref, small_ref, cw_ref, cb_ref, exp_ref,
                dtb_ref, a_ref, dskip_ref, o_ref, ext_ref, s_ref, *, reverse):
    i = pl.program_id(0)
    n = pl.num_programs(0)
    blk = (n - 1 - i) if reverse else i
    T = CHUNK

    @pl.when(i == 0)
    def _():
        s_ref[...] = jnp.zeros_like(s_ref)

    xbc = _conv_halo(prev_ref, main_ref, next_ref, cw_ref, ext_ref, blk == 0, blk == n - 1, SSM_CONV)
    xbc = _silu(xbc + cb_ref[...])
    xs = xbc[:, 0:GROUP_W]
    dt = _softplus(_dot_split(small_ref[...], exp_ref[...], 3) + dtb_ref[...])
    a = dt * a_ref[...]
    xdt = xs * dt
    _, _, vis, trif = _scan_masks(reverse)
    hms = _head_masks(T)
    prep = []
    for s in range(SCAN_CHUNKS):
        r0, r1 = s * T, (s + 1) * T
        a_cum = _dot_split_rhs(trif, a[r0:r1], 3)
        a_tot = jnp.sum(a[r0:r1], axis=0, keepdims=True)
        xdt_s = xdt[r0:r1]
        scores, c_exp, b_dec = [], [], []
        for g in range(2):
            bm = xbc[r0:r1, GROUP_W + g * SSM_STATE:GROUP_W + (g + 1) * SSM_STATE]
            cm = xbc[r0:r1, 2 * GROUP_W + g * SSM_STATE:2 * GROUP_W + (g + 1) * SSM_STATE]
            cb = _dot_nt(cm, bm)
            for h in (2 * g, 2 * g + 1):
                col = jnp.broadcast_to(a_cum[:, h * HEAD_DIM:h * HEAD_DIM + 1], (T, T))
                tot = jnp.broadcast_to(a_tot[:, h * HEAD_DIM:h * HEAD_DIM + 1], (T, T))
                scores.append(cb * jnp.where(vis, jnp.exp(col - col.T), 0.0))
                c_exp.append(cm * jnp.exp(col))
                b_dec.append(bm * jnp.exp(tot - col))
        y_in = _dot(jnp.concatenate(scores, axis=1), _stack_heads(xdt_s, hms))
        prep.append((y_in, jnp.concatenate(c_exp, axis=1), jnp.concatenate(b_dec, axis=1), xdt_s,
                     jnp.exp(a_tot)))
    state = s_ref[...]
    for s in (reversed(range(SCAN_CHUNKS)) if reverse else range(SCAN_CHUNKS)):
        y, c_cat, b_cat, xdt_s, decay = prep[s]
        y = y + _dot(c_cat, _stack_heads(state, hms))
        upd_all = _dot_tn(b_cat, xdt_s)
        upd = jnp.zeros((SSM_STATE, GROUP_W), F32)
        for h in range(N_HEADS):
            upd = upd + jnp.where(hms[h], upd_all[h * SSM_STATE:(h + 1) * SSM_STATE], 0.0)
        state = decay * state + upd
        if not reverse:
            y = y + xs[s * T:(s + 1) * T] * dskip_ref[...]
        o_ref[s * T:(s + 1) * T, :] = y
    s_ref[...] = state


def _ssd(cols_ssm, cols_small, conv_w, conv_b, expand, dt_bias, a_neg, d_skip, reverse):
    L = cols_ssm.shape[0]
    rows = SCAN_CHUNKS * CHUNK
    nblk = L // rows
    prev, main, nxt, blk = _halo_specs(rows, XBC_W, nblk, reverse)
    const = lambda a: pl.BlockSpec(a.shape, lambda i: (0,) * a.ndim)
    return pl.pallas_call(
        functools.partial(_ssd_kernel, reverse=reverse),
        grid=(nblk,),
        in_specs=[prev, main, nxt, pl.BlockSpec((rows, 128), lambda i: (blk(i), 0)), const(conv_w),
                  const(conv_b), const(expand), const(dt_bias), const(a_neg), const(d_skip)],
        out_specs=pl.BlockSpec((rows, GROUP_W), lambda i: (blk(i), 0)),
        out_shape=jax.ShapeDtypeStruct((L, GROUP_W), F32),
        scratch_shapes=[pltpu.VMEM((rows + 2 * HALO, XBC_W), F32),
                        pltpu.VMEM((SSM_STATE, GROUP_W), F32)],
        compiler_params=_params(("arbitrary",)),
        name="ssd_bwd" if reverse else "ssd_fwd",
    )(cols_ssm, cols_ssm, cols_ssm, cols_small, conv_w, conv_b, expand, dt_bias, a_neg, d_skip)


def _unit_tri_inverse_pairs(a_list, eye2, bd16, offs):
    n1 = [jnp.where(bd16, -a, 0.0) for a in a_list]
    t = [eye2 + x for x in n1]
    p = [_dot(x, _pair_diag(x)) for x in n1]
    for lvl in range(3):
        pd = [_pair_diag(x) for x in p]
        t = [x + _dot(x, d) for x, d in zip(t, pd)]
        if lvl < 2:
            p = [_dot(x, d) for x, d in zip(p, pd)]
    for off in offs:
        m = [_dot(x, _pair_diag(jnp.where(off, a, 0.0))) for x, a in zip(t, a_list)]
        t = [x - _dot(y, _pair_diag(x)) for x, y in zip(t, m)]
    return t


def _gdn_kernel(prev_ref, main_ref, next_ref, small_ref, cw_ref, expa_ref, expb_ref, dtb_ref, a_ref,
                o_ref, ext_ref, s_ref, *, reverse):
    i = pl.program_id(0)
    n = pl.num_programs(0)
    blk = (n - 1 - i) if reverse else i
    T = CHUNK

    @pl.when(i == 0)
    def _():
        s_ref[...] = jnp.zeros_like(s_ref)

    qkv = _silu(_conv_halo(prev_ref, main_ref, next_ref, cw_ref, ext_ref, blk == 0, blk == n - 1, GDN_CONV))
    bdf = jnp.where(_block_diag_mask(), 1.0, 0.0)
    q = qkv[:, 0:GROUP_W]
    k = qkv[:, GROUP_W:2 * GROUP_W]
    v = qkv[:, 2 * GROUP_W:3 * GROUP_W]
    q = q * lax.rsqrt(_dot_split(q * q, bdf, 2) + EPS) * (HEAD_DIM ** -0.5)
    k = k * lax.rsqrt(_dot_split(k * k, bdf, 2) + EPS)
    small = small_ref[...]
    g = a_ref[...] * _softplus(_dot_split(small, expa_ref[...], 3) + dtb_ref[...])
    beta = _sigmoid(_dot_split(small, expb_ref[...], 3))
    ri, ci, vis, trif = _scan_masks(reverse)
    strict = vis & (ri != ci)
    r2 = lax.broadcasted_iota(I32, (T, 2 * T), 0)
    c2 = lax.broadcasted_iota(I32, (T, 2 * T), 1) % T
    eye2 = jnp.where(r2 == c2, 1.0, 0.0)
    bd16 = (r2 // 16) == (c2 // 16)
    offs = []
    for sz in (16, 32, 64):
        pair = (r2 // (2 * sz)) == (c2 // (2 * sz))
        later, earlier = (c2, r2) if reverse else (r2, c2)
        offs.append(pair & ((later // sz) % 2 == 1) & ((earlier // sz) % 2 == 0))
    kb = k * beta
    vb = v * beta
    hms = _head_masks(T)
    pre, a_pairs = [], []
    for s in range(SCAN_CHUNKS):
        r0, r1 = s * T, (s + 1) * T
        gc = _dot_split_rhs(trif, g[r0:r1], 3)
        g_tot = jnp.sum(g[r0:r1], axis=0, keepdims=True)
        egc = jnp.exp(gc)
        k_s, q_s = k[r0:r1], q[r0:r1]
        a_all = _dot_nt(_stack_heads(kb[r0:r1], hms), k_s)
        q_all = _dot_nt(_stack_heads(q_s, hms), k_s)
        a_h, qk_h = [], []
        for h in range(N_HEADS):
            col = jnp.broadcast_to(gc[:, h * HEAD_DIM:h * HEAD_DIM + 1], (T, T))
            decay = jnp.where(vis, jnp.exp(col - col.T), 0.0)
            a_h.append(jnp.where(strict, a_all[h * T:(h + 1) * T] * decay, 0.0))
            qk_h.append(q_all[h * T:(h + 1) * T] * decay)
        for pr in range(N_HEADS // 2):
            a_pairs.append(jnp.concatenate(a_h[2 * pr:2 * pr + 2], axis=1))
        rhs = jnp.concatenate([_stack_heads(vb[r0:r1], hms), _stack_heads(kb[r0:r1] * egc, hms)], axis=1)
        pre.append((jnp.concatenate(qk_h, axis=1), rhs, q_s * egc, k_s * jnp.exp(g_tot - gc), jnp.exp(g_tot)))
    t_pairs = _unit_tri_inverse_pairs(a_pairs, eye2, bd16, offs)
    npair = N_HEADS // 2
    uw = [_dot(jnp.concatenate(t_pairs[s * npair:(s + 1) * npair], axis=1), pre[s][1])
          for s in range(SCAN_CHUNKS)]
    state = s_ref[...]
    for s in (reversed(range(SCAN_CHUNKS)) if reverse else range(SCAN_CHUNKS)):
        qk_cat, _, q_dec, k_dec, decay = pre[s]
        v_new = uw[s][:, 0:GROUP_W] - _dot(uw[s][:, GROUP_W:2 * GROUP_W], state)
        o_ref[s * T:(s + 1) * T, :] = _dot(q_dec, state) + _dot(qk_cat, _stack_heads(v_new, hms))
        state = state * decay + bdf * _dot_tn(k_dec, v_new)
    s_ref[...] = state


def _gdn(cols_gdn, cols_small, conv_w, exp_a, exp_b, dt_bias, a_neg, reverse):
    L = cols_gdn.shape[0]
    rows = SCAN_CHUNKS * CHUNK
    nblk = L // rows
    prev, main, nxt, blk = _halo_specs(rows, XBC_W, nblk, reverse)
    const = lambda a: pl.BlockSpec(a.shape, lambda i: (0,) * a.ndim)
    return pl.pallas_call(
        functools.partial(_gdn_kernel, reverse=reverse),
        grid=(nblk,),
        in_specs=[prev, main, nxt, pl.BlockSpec((rows, 128), lambda i: (blk(i), 0)), const(conv_w),
                  const(exp_a), const(exp_b), const(dt_bias), const(a_neg)],
        out_specs=pl.BlockSpec((rows, GROUP_W), lambda i: (blk(i), 0)),
        out_shape=jax.ShapeDtypeStruct((L, GROUP_W), F32),
        scratch_shapes=[pltpu.VMEM((rows + 2 * HALO, XBC_W), F32),
                        pltpu.VMEM((GROUP_W, GROUP_W), F32)],
        compiler_params=_params(("arbitrary",)),
        name="gdn_bwd" if reverse else "gdn_fwd",
    )(cols_gdn, cols_gdn, cols_gdn, cols_small, conv_w, exp_a, exp_b, dt_bias, a_neg)


def _hy_pre_kernel(prev_ref, main_ref, next_ref, cw_ref, cb_ref, vx_ref, x0_ref, ext_ref):
    i = pl.program_id(0)
    n = pl.num_programs(0)
    u = _conv_halo(prev_ref, main_ref, next_ref, cw_ref, ext_ref, i == 0, i == n - 1, HY_CONV) + cb_ref[...]
    x0_ref[...] = u[:, 0:GROUP_W]
    vx_ref[...] = u[:, 2 * GROUP_W:3 * GROUP_W] * u[:, GROUP_W:2 * GROUP_W]


def _hy_pre(cols_hy, conv_w, conv_b, tm=512):
    L = cols_hy.shape[0]
    tm = min(tm, L)
    w = 3 * GROUP_W
    prev, main, nxt, _ = _halo_specs(tm, w, L // tm, False)
    const = lambda a: pl.BlockSpec(a.shape, lambda i: (0,) * a.ndim)
    return pl.pallas_call(
        _hy_pre_kernel,
        grid=(L // tm,),
        in_specs=[prev, main, nxt, const(conv_w), const(conv_b)],
        out_specs=[pl.BlockSpec((tm, GROUP_W), lambda i: (i, 0))] * 2,
        out_shape=[jax.ShapeDtypeStruct((L, GROUP_W), F32)] * 2,
        scratch_shapes=[pltpu.VMEM((tm + 2 * HALO, w), F32)],
        compiler_params=_params(("parallel",)),
        name="hyena_pre",
    )(cols_hy, cols_hy, cols_hy, conv_w, conv_b)


def _hy_filter_kernel(z_ref, win_ref, bin_ref, wmid_ref, bmid_ref, freq_ref, wout_ref, delta_ref,
                      f_ref, *, seq_len):
    i = pl.program_id(0)
    tm = z_ref.shape[0]
    half = tm // 2
    z = z_ref[...]
    zz = jnp.concatenate([z[0:half], z[half:tm]], axis=1)
    freq = freq_ref[...]
    h = jnp.sin(freq * (_dot_hi(zz, win_ref[...]) + bin_ref[...]))
    for j in range(HY_INNER):
        h = jnp.sin(freq * (_dot_hi(h, wmid_ref[j]) + bmid_ref[j]))
    row = i * tm + lax.broadcasted_iota(I32, (half, 1), 0)
    for part in range(2):
        out = _dot_hi(h[:, part * HY_ORDER:(part + 1) * HY_ORDER], wout_ref[...])
        t = z[part * half:(part + 1) * half, 0:1]
        out = out * (jnp.exp(-t * delta_ref[...]) + HY_SHIFT)
        f_ref[part * half:(part + 1) * half, :] = jnp.where(row + part * half == seq_len, 0.0, out)


def _hy_filter(z_ext, w_in2, b_in2, w_mid2, b_mid2, freq2, w_out, abs_delta, tm=512):
    n = z_ext.shape[0]
    L = n // 2
    tm = min(tm, L)
    nblk = n // tm
    const = lambda a: pl.BlockSpec(a.shape, lambda i: (0,) * a.ndim)
    return pl.pallas_call(
        functools.partial(_hy_filter_kernel, seq_len=L),
        grid=(nblk,),
        in_specs=[pl.BlockSpec((tm, HY_EMB_PAD), lambda i: (i, 0)), const(w_in2), const(b_in2),
                  const(w_mid2), const(b_mid2), const(freq2),
                  pl.BlockSpec((HY_ORDER, GROUP_W), lambda i: (0, i // (nblk // 2))), const(abs_delta)],
        out_specs=pl.BlockSpec((tm, GROUP_W), lambda i: (i, 0)),
        out_shape=jax.ShapeDtypeStruct((n, GROUP_W), F32),
        compiler_params=_params(("parallel",)),
        name="hyena_filter",
    )(z_ext, w_in2, b_in2, w_mid2, b_mid2, freq2, w_out, abs_delta)


def _left_matmul_kernel(m_ref, x_ref, o_ref):
    o_ref[...] = _dot(m_ref[...], x_ref[...]).astype(o_ref.dtype)


def _left_matmul(m, x2d, out_dtype, cb=4096):
    rows, kdim = m.shape
    ncol = x2d.shape[1]
    cb = min(cb, ncol)
    return pl.pallas_call(
        _left_matmul_kernel,
        grid=(ncol // cb,),
        in_specs=[pl.BlockSpec((rows, kdim), lambda i: (0, 0)), pl.BlockSpec((kdim, cb), lambda i: (0, i))],
        out_specs=pl.BlockSpec((rows, cb), lambda i: (0, i)),
        out_shape=jax.ShapeDtypeStruct((rows, ncol), out_dtype),
        compiler_params=_params(("parallel",)),
        name="hyena_outer_dft",
    )(m, x2d)


def _twiddle_inner_dft(a_ref, twc_ref, tws_ref, g_ref, kk):
    ar = a_ref[0, kk].astype(F32)
    ai = a_ref[1, kk].astype(F32)
    cs = twc_ref[kk]
    sn = tws_ref[kk]
    br = ar * cs + ai * sn
    bi = ai * cs - ar * sn
    x = _dot(g_ref[...], jnp.concatenate([br, bi], axis=0))
    return x[0:FFT_N2], x[FFT_N2:2 * FFT_N2]


def _hy_filter_spec_kernel(a_ref, twc_ref, tws_ref, g_ref, x_ref):
    for kk in range(a_ref.shape[1]):
        xr, xi = _twiddle_inner_dft(a_ref, twc_ref, tws_ref, g_ref, kk)
        x_ref[0, kk] = xr.astype(x_ref.dtype)
        x_ref[1, kk] = xi.astype(x_ref.dtype)


def _hy_conv_spec_kernel(a_ref, twc_ref, tws_ref, g_ref, gi_ref, f_ref, e_ref):
    for kk in range(a_ref.shape[1]):
        xr, xi = _twiddle_inner_dft(a_ref, twc_ref, tws_ref, g_ref, kk)
        fr = f_ref[0, kk].astype(F32)
        fi = f_ref[1, kk].astype(F32)
        yr = xr * fr - xi * fi
        yi = xr * fi + xi * fr
        dd = _dot(gi_ref[...], jnp.concatenate([yr, yi], axis=0))
        dr = dd[0:FFT_N2]
        di = dd[FFT_N2:2 * FFT_N2]
        cs = twc_ref[kk]
        sn = tws_ref[kk]
        e_ref[0, kk] = (dr * cs - di * sn).astype(e_ref.dtype)
        e_ref[1, kk] = (dr * sn + di * cs).astype(e_ref.dtype)


def _hy_spec_specs():
    blk = pl.BlockSpec((2, FFT_K1_STEP, FFT_N2, GROUP_W), lambda k: (0, k, 0, 0))
    tw = pl.BlockSpec((FFT_K1_STEP, FFT_N2, 1), lambda k: (k, 0, 0))
    mat = pl.BlockSpec((2 * FFT_N2, 2 * FFT_N2), lambda k: (0, 0))
    return blk, tw, mat


def _hy_filter_spec(a4, twc, tws, g):
    blk, tw, mat = _hy_spec_specs()
    return pl.pallas_call(
        _hy_filter_spec_kernel,
        grid=(a4.shape[1] // FFT_K1_STEP,),
        in_specs=[blk, tw, tw, mat],
        out_specs=blk,
        out_shape=jax.ShapeDtypeStruct(a4.shape, BF16),
        compiler_params=_params(("parallel",)),
        name="hyena_filter_spectrum",
    )(a4, twc, tws, g)


def _hy_conv_spec(a4, twc, tws, g, gi, fspec):
    blk, tw, mat = _hy_spec_specs()
    return pl.pallas_call(
        _hy_conv_spec_kernel,
        grid=(a4.shape[1] // FFT_K1_STEP,),
        in_specs=[blk, tw, tw, mat, mat, blk],
        out_specs=blk,
        out_shape=jax.ShapeDtypeStruct(a4.shape, BF16),
        compiler_params=_params(("parallel",)),
        name="hyena_spectral_product",
    )(a4, twc, tws, g, gi, fspec)


def _postmix_kernel(h_ref, retf_ref, retb_ref, g_ref, ssdf_ref, ssdb_ref, zs_ref, hy_ref, vx_ref, x0_ref,
                    gdnf_ref, gdnb_ref, zg_ref, gnw_ref, snw_ref, hyb_ref, hnw_ref, dnw_ref,
                    wout_ref, fnw_ref, rw_ref, h1_ref, xn_ref, aff_ref):
    avg = jnp.where(_block_diag_mask(), 1.0 / HEAD_DIM, 0.0)
    o = retf_ref[...] + retb_ref[...]
    mu = _dot_split(o, avg, 2)
    var = _dot_split(jnp.square(o - mu), avg, 2)
    m_ret = _silu(g_ref[...]) * ((o - mu) * lax.rsqrt(var + EPS) * gnw_ref[...])

    y = (ssdf_ref[...] + ssdb_ref[...]) * _silu(zs_ref[...])
    m_ssm = y * lax.rsqrt(jnp.mean(y * y, axis=-1, keepdims=True) + EPS) * snw_ref[...]

    t = (hy_ref[...] + vx_ref[...] * hyb_ref[...]) * x0_ref[...]
    m_hy = t * lax.rsqrt(jnp.mean(t * t, axis=-1, keepdims=True) + EPS) * hnw_ref[...]

    o = gdnf_ref[...] + gdnb_ref[...]
    m_gdn = o * lax.rsqrt(_dot_split(o * o, avg, 2) + EPS) * dnw_ref[...] * _silu(zg_ref[...])

    h1 = h_ref[...]
    for gi, m in enumerate((m_ret, m_ssm, m_hy, m_gdn)):
        h1 = h1 + jnp.dot(m.astype(BF16), wout_ref[gi * GROUP_W:(gi + 1) * GROUP_W, :],
                          preferred_element_type=F32)
    h1_ref[...] = h1
    xn = h1 * lax.rsqrt(jnp.mean(h1 * h1, axis=-1, keepdims=True) + EPS) * fnw_ref[...]
    xn_ref[...] = xn.astype(BF16)
    logits = _dot_3pass(xn, rw_ref[...])
    logits = logits - jnp.max(logits, axis=-1, keepdims=True)
    ex = jnp.exp(logits)
    aff_ref[...] = ex / jnp.sum(ex, axis=-1, keepdims=True)


def _postmix(h, ret_f, ret_b, cols_ret, ssd_f, ssd_b, cols_ssm, hy_y, hy_vx, hy_x0, gdn_f, gdn_b, cols_gdn,
             gn_w, ssm_nw, hy_bias, hy_nw, gdn_nw, w_out, ffn_nw, router_w, tm=256):
    L = h.shape[0]
    tm = min(tm, L)
    row = lambda n: pl.BlockSpec((tm, n), lambda i: (i, 0))
    grp = row(GROUP_W)
    zcol = pl.BlockSpec((tm, GROUP_W), lambda i: (i, 3))
    const = lambda a: pl.BlockSpec(a.shape, lambda i: (0,) * a.ndim)
    return pl.pallas_call(
        _postmix_kernel,
        grid=(L // tm,),
        in_specs=[row(D_MODEL), grp, grp, zcol, grp, grp, zcol, grp, grp, grp, grp, grp, zcol,
                  const(gn_w), const(ssm_nw), const(hy_bias), const(hy_nw), const(gdn_nw),
                  const(w_out), const(ffn_nw), const(router_w)],
        out_specs=[row(D_MODEL), row(D_MODEL), row(N_EXPERTS)],
        out_shape=[jax.ShapeDtypeStruct((L, D_MODEL), F32), jax.ShapeDtypeStruct((L, D_MODEL), BF16),
                   jax.ShapeDtypeStruct((L, N_EXPERTS), F32)],
        compiler_params=_params(("parallel",)),
        name="postmix",
    )(h, ret_f, ret_b, cols_ret, ssd_f, ssd_b, cols_ssm, hy_y, hy_vx, hy_x0, gdn_f, gdn_b, cols_gdn,
      gn_w, ssm_nw, hy_bias, hy_nw, gdn_nw, w_out, ffn_nw, router_w)


def _topk_kernel(aff_ref, sel_ref, *, cap, idx_bits):
    bits = lax.bitcast_convert_type(aff_ref[...], I32)
    shape = bits.shape
    capf = float(cap)

    def count(mask):
        return jnp.sum(jnp.where(mask, 1.0, 0.0), axis=1, keepdims=True)

    def value_step(i, thr):
        cand = thr | lax.shift_left(jnp.int32(1), 30 - i)
        return jnp.where(count(bits >= cand) >= capf, cand, thr)

    thr = lax.fori_loop(0, 31, value_step, jnp.zeros((shape[0], 1), I32))
    above = bits > thr
    tied = bits == thr
    need = capf - count(above)
    idx = lax.broadcasted_iota(I32, shape, 1)

    def index_step(i, m):
        cand = m | lax.shift_left(jnp.int32(1), idx_bits - 1 - i)
        return jnp.where(count(tied & (idx < cand)) < need, cand, m)

    last = lax.fori_loop(0, idx_bits, index_step, jnp.zeros((shape[0], 1), I32))
    sel_ref[...] = jnp.where(above | (tied & (idx <= last)), 1.0, 0.0)


def _topk(aff_t, cap):
    n_e, L = aff_t.shape
    return pl.pallas_call(
        functools.partial(_topk_kernel, cap=cap, idx_bits=int(math.log2(L)) + 1),
        out_shape=jax.ShapeDtypeStruct((n_e, L), F32),
        compiler_params=_params(None),
        name="expert_topk",
    )(aff_t)


def _positions_kernel(sel_ref, posm_ref, offs_ref, *, n_e):
    s = sel_ref[...]
    nb = s.shape[0] // n_e
    r = lax.broadcasted_iota(I32, (128, 128), 0)
    c = lax.broadcasted_iota(I32, (128, 128), 1)
    incl = _dot(s, jnp.where(r <= c, 1.0, 0.0))
    tot = _dot(s, jnp.ones((128, 128), F32))
    rb = lax.broadcasted_iota(I32, (nb, nb), 0)
    cbk = lax.broadcasted_iota(I32, (nb, nb), 1)
    before = jnp.where(cbk < rb, 1.0, 0.0)
    for e in range(n_e):
        offs = _dot(before, tot[e * nb:(e + 1) * nb])
        se = s[e * nb:(e + 1) * nb]
        pos = incl[e * nb:(e + 1) * nb] - se + offs
        posm_ref[e * nb:(e + 1) * nb, :] = jnp.where(se > 0.5, pos, -1.0).astype(I32)
        offs_ref[e * nb:(e + 1) * nb, :] = offs.astype(I32)


def _positions(sel2, n_e):
    return pl.pallas_call(
        functools.partial(_positions_kernel, n_e=n_e),
        out_shape=[jax.ShapeDtypeStruct(sel2.shape, I32)] * 2,
        compiler_params=_params(None),
        name="expert_positions",
    )(sel2)


def _window_hits(posm_row, lo, cap):
    base = pl.multiple_of(jnp.minimum(lo, cap - ROW_TILE), ROW_ALIGN)
    rows = base + lax.broadcasted_iota(I32, (ROW_TILE, posm_row.shape[1]), 0)
    return base, (posm_row == rows) & (rows >= lo)


def _gather_kernel(off_ref, lo_ref, b_ref, posm_ref, aff_ref, x_ref, o_ref, og_ref):
    e = pl.program_id(0)
    nblk, _, tb = posm_ref.shape
    cap = o_ref.shape[0]
    o_ref[...] = jnp.zeros_like(o_ref)
    og_ref[...] = jnp.zeros_like(og_ref)

    def select(i):
        b = jnp.minimum(b_ref[i], nblk - 1)
        base, hit = _window_hits(posm_ref[b], lo_ref[i], cap)
        x_blk = x_ref[pl.ds(pl.multiple_of(b * tb, tb), tb), :]
        rows = jnp.dot(jnp.where(hit, 1.0, 0.0).astype(BF16), x_blk, preferred_element_type=F32).astype(BF16)
        gate = jnp.sum(jnp.where(hit, aff_ref[b], 0.0), axis=1, keepdims=True)
        return base, rows, gate

    def item_pair(ip, carry):
        picked = [select(2 * ip), select(2 * ip + 1)]
        for base, rows, gate in picked:
            o_ref[pl.ds(base, ROW_TILE), :] += rows
            og_ref[pl.ds(base, ROW_TILE), :] += gate
        return carry

    lax.fori_loop(off_ref[e] // 2, off_ref[e + 1] // 2, item_pair, 0)


def _gather(work, posm4, aff4, xn, cap):
    n_e, nblk, _, tb = posm4.shape
    per_expert = pl.BlockSpec((None, nblk, 1, tb), lambda e, *_: (e, 0, 0, 0))
    return pl.pallas_call(
        _gather_kernel,
        grid_spec=pltpu.PrefetchScalarGridSpec(
            num_scalar_prefetch=3,
            grid=(n_e,),
            in_specs=[per_expert, per_expert, pl.BlockSpec(memory_space=pltpu.VMEM)],
            out_specs=[pl.BlockSpec((None, cap, D_MODEL), lambda e, *_: (e, 0, 0)),
                       pl.BlockSpec((None, cap, 1), lambda e, *_: (e, 0, 0))],
        ),
        out_shape=[jax.ShapeDtypeStruct((n_e, cap, D_MODEL), BF16),
                   jax.ShapeDtypeStruct((n_e, cap, 1), F32)],
        compiler_params=_params(("arbitrary",), VMEM_LIMIT_RESIDENT),
        name="expert_gather",
    )(*work, posm4, aff4, xn)


def _ffn_kernel(x_ref, gate_ref, wg_ref, wu_ref, wd_ref, o_ref, acc_ref):
    f = pl.program_id(1)

    @pl.when(f == 0)
    def _():
        acc_ref[...] = jnp.zeros_like(acc_ref)

    x = x_ref[...]
    gate = jnp.dot(x, wg_ref[...].astype(BF16), preferred_element_type=F32)
    up = jnp.dot(x, wu_ref[...].astype(BF16), preferred_element_type=F32)
    acc_ref[...] += _dot(_silu(gate) * up, wd_ref[...])

    @pl.when(f == pl.num_programs(1) - 1)
    def _():
        o_ref[...] = (acc_ref[...] * gate_ref[...]).astype(BF16)


def _expert_ffn(xs, gate_c, w_gate, w_up, w_down, layer, group, fb=256):
    n_e, cap, _ = xs.shape
    n_g = n_e // EXPERT_GROUPS
    e0 = group * n_g
    return pl.pallas_call(
        _ffn_kernel,
        grid=(n_g, EXPERT_FF // fb),
        in_specs=[pl.BlockSpec((None, cap, D_MODEL), lambda e, f: (e0 + e, 0, 0)),
                  pl.BlockSpec((None, cap, 1), lambda e, f: (e0 + e, 0, 0)),
                  pl.BlockSpec((None, None, D_MODEL, fb), lambda e, f: (layer, e0 + e, 0, f)),
                  pl.BlockSpec((None, None, D_MODEL, fb), lambda e, f: (layer, e0 + e, 0, f)),
                  pl.BlockSpec((None, None, fb, D_MODEL), lambda e, f: (layer, e0 + e, f, 0))],
        out_specs=pl.BlockSpec((None, cap, D_MODEL), lambda e, f: (e, 0, 0)),
        out_shape=jax.ShapeDtypeStruct((n_g, cap, D_MODEL), BF16),
        scratch_shapes=[pltpu.VMEM((cap, D_MODEL), F32)],
        compiler_params=_params(("parallel", "arbitrary")),
        name="expert_ffn",
    )(xs, gate_c, w_gate, w_up, w_down)


def _scatter_kernel(off_ref, lo_ref, e_ref, posm_ref, y_ref, o_ref):
    b = pl.program_id(0)
    n_g, cap, _ = y_ref.shape
    o_ref[...] = jnp.zeros_like(o_ref)

    def pick(i):
        e = jnp.minimum(e_ref[i], n_g - 1)
        base, hit = _window_hits(posm_ref[e], lo_ref[i], cap)
        return jnp.where(hit, 1.0, 0.0).astype(BF16), y_ref[e, pl.ds(base, ROW_TILE), :]

    def item_pair(ip, carry):
        p0, y0 = pick(2 * ip)
        p1, y1 = pick(2 * ip + 1)
        o_ref[...] += lax.dot_general(jnp.concatenate([p0, p1], axis=0), jnp.concatenate([y0, y1], axis=0),
                                      (((0,), (0,)), ((), ())), preferred_element_type=F32)
        return carry

    lax.fori_loop(off_ref[b] // 2, off_ref[b + 1] // 2, item_pair, 0)


def _scatter(work, posm4, ye, group):
    n_e, nblk, _, tb = posm4.shape
    n_g = ye.shape[0]
    return pl.pallas_call(
        _scatter_kernel,
        grid_spec=pltpu.PrefetchScalarGridSpec(
            num_scalar_prefetch=3,
            grid=(nblk,),
            in_specs=[pl.BlockSpec((n_g, None, 1, tb), lambda b, *_: (group, b, 0, 0)),
                      pl.BlockSpec(memory_space=pltpu.VMEM)],
            out_specs=pl.BlockSpec((tb, D_MODEL), lambda b, *_: (b, 0)),
        ),
        out_shape=jax.ShapeDtypeStruct((nblk * tb, D_MODEL), F32),
        compiler_params=_params(("arbitrary",), VMEM_LIMIT_RESIDENT),
        name="expert_scatter",
    )(*work, posm4, ye)


def _enumerate_items(n_items, first_row, cap, n_work):
    n_groups, n_members = n_items.shape
    parity = jnp.sum(n_items, axis=1, keepdims=True) % 2
    flat = jnp.concatenate([n_items, parity], axis=1).reshape(-1)
    first = jnp.concatenate([first_row, jnp.full((n_groups, 1), cap, I32)], axis=1).reshape(-1)
    ends = jnp.cumsum(flat)
    begins = ends - flat
    w = jnp.arange(n_work, dtype=I32)
    pair = jnp.sum((ends[None, :] <= w[:, None]).astype(I32), axis=1)
    onehot = pair[:, None] == jnp.arange(flat.shape[0], dtype=I32)[None, :]
    begin_w = jnp.sum(jnp.where(onehot, begins[None, :], 0), axis=1)
    first_w = jnp.sum(jnp.where(onehot, first[None, :], 0), axis=1)
    offsets = jnp.concatenate([jnp.zeros((1,), I32), ends.reshape(n_groups, n_members + 1)[:, -1]])
    lo = first_w + ROW_TILE * (w - begin_w)
    return offsets.astype(I32), lo.astype(I32), (pair % (n_members + 1)).astype(I32)


def _work_lists(offs, cap, tb):
    n_e, nb128 = offs.shape
    per = tb // 128
    nblk = nb128 // per
    start = offs[:, ::per]
    end = jnp.concatenate([start[:, 1:], jnp.full((n_e, 1), cap, I32)], axis=1)
    first_row = (start // ROW_ALIGN) * ROW_ALIGN
    n_items = jnp.where(end > start, (end - first_row + ROW_TILE - 1) // ROW_TILE, 0)
    per_expert = (cap + (ROW_TILE + ROW_ALIGN - 2) * nblk) // ROW_TILE + 2
    off_g, lo_g, b_g = _enumerate_items(n_items, first_row, cap, n_e * per_expert)
    n_g = n_e // EXPERT_GROUPS
    scatter_work = []
    for grp in range(EXPERT_GROUPS):
        sl = slice(grp * n_g, (grp + 1) * n_g)
        scatter_work.append(_enumerate_items(n_items[sl].T, first_row[sl].T, cap, n_g * per_expert + nblk))
    return (off_g, lo_g, b_g), scatter_work


def _ple_kernel(h_ref, moe_a_ref, moe_b_ref, p_ref, nw_ref, wg_ref, wp_ref, fw_ref, o_ref, *, final):
    h2 = h_ref[...] + (moe_a_ref[...] + moe_b_ref[...])
    hn = h2 * lax.rsqrt(jnp.mean(h2 * h2, axis=-1, keepdims=True) + EPS) * nw_ref[...]
    gate = _sigmoid(jnp.dot(hn.astype(BF16), wg_ref[...], preferred_element_type=F32))
    h3 = h2 + gate * jnp.dot(p_ref[...].astype(BF16), wp_ref[...], preferred_element_type=F32)
    if final:
        h3 = h3 * lax.rsqrt(jnp.mean(h3 * h3, axis=-1, keepdims=True) + EPS) * fw_ref[...]
    o_ref[...] = h3


def _ple(h1, moe_a, moe_b, p, layer, norm_w, w_gate, w_proj, final_w, final, tm=512):
    L = h1.shape[0]
    tm = min(tm, L)
    row = lambda n: pl.BlockSpec((tm, n), lambda i: (i, 0))
    const = lambda a: pl.BlockSpec(a.shape, lambda i: (0,) * a.ndim)
    return pl.pallas_call(
        functools.partial(_ple_kernel, final=final),
        grid=(L // tm,),
        in_specs=[row(D_MODEL), row(D_MODEL), row(D_MODEL),
                  pl.BlockSpec((None, None, tm, PLE_DIM), lambda i: (layer, 0, i, 0)),
                  const(norm_w), const(w_gate), const(w_proj), const(final_w)],
        out_specs=row(D_MODEL),
        out_shape=jax.ShapeDtypeStruct((L, D_MODEL), F32),
        compiler_params=_params(("parallel",)),
        name="ple",
    )(h1, moe_a, moe_b, p, norm_w, w_gate, w_proj, final_w)


def _expand_table(base):
    t = np.zeros((2, 128, GROUP_W), np.float32)
    for d in range(2):
        for h in range(N_HEADS):
            t[d, base + N_HEADS * d + h, h * HEAD_DIM:(h + 1) * HEAD_DIM] = 1.0
    return t


def _retention_tables():
    hh = np.arange(N_HEADS, dtype=np.float64)
    lg = np.log(1.0 - 2.0 ** (-5.0 - hh))
    t = np.arange(CHUNK, dtype=np.float64)
    dmask = np.exp(np.abs(t[:, None] - t[None, :])[None] * lg[:, None, None])
    lanes = np.repeat(lg, HEAD_DIM)[None, :]
    pwt = np.exp((CHUNK - t)[:, None] * lanes)
    pwh = np.exp(t[:, None] * lanes)
    dec = np.exp(CHUNK * lanes)
    return [a.astype(np.float32) for a in (dmask, pwh, pwt, dec)]


def _dft_tables(n1):
    n = n1 * FFT_N2
    nh = n1 // 2 + 1
    nhp = -(-nh // FFT_K1_STEP) * FFT_K1_STEP
    k1 = np.arange(nh, dtype=np.float64)
    t1 = np.arange(n1, dtype=np.float64)
    ang1 = 2.0 * np.pi * np.outer(k1, t1) / n1
    f1 = np.zeros((2 * nhp, n1))
    f1[0:nh] = np.cos(ang1)
    f1[nhp:nhp + nh] = -np.sin(ang1)
    weight = np.where((k1 == 0) | (k1 == n1 // 2), 1.0, 2.0)[None, :] / n
    f3 = np.zeros((n1 // 2, 2 * nhp))
    f3[:, 0:nh] = np.cos(ang1.T[:n1 // 2]) * weight
    f3[:, nhp:nhp + nh] = -np.sin(ang1.T[:n1 // 2]) * weight
    k2 = np.arange(FFT_N2, dtype=np.float64)
    ang2 = 2.0 * np.pi * np.outer(k2, k2) / FFT_N2
    c2, s2 = np.cos(ang2), np.sin(ang2)
    g = np.block([[c2, s2], [-s2, c2]])
    gi = np.block([[c2, -s2], [s2, c2]])
    angt = np.zeros((nhp, FFT_N2))
    angt[0:nh] = 2.0 * np.pi * np.outer(k1, k2) / n
    twc, tws = np.cos(angt)[:, :, None], np.sin(angt)[:, :, None]
    return [a.astype(np.float32) for a in (f1, f3, g, gi, twc, tws)]


def _hyena_positions(L):
    n = jnp.arange(2 * L, dtype=I32)
    m = jnp.where(n <= L, n, 2 * L - n).astype(F32)[:, None]
    t = m / (L - 1)
    bands = jnp.linspace(1e-4, HY_BANDS - 1, HY_BANDS, dtype=F32)
    ang = (2.0 * math.pi / L) * m * bands[None]
    return jnp.concatenate([t, jnp.cos(ang), -jnp.sin(ang), jnp.zeros((2 * L, HY_EMB_PAD - HY_EMB), F32)], axis=-1)


def _rope_tables(L):
    inv = ROPE_BASE ** (-jnp.arange(0, HEAD_DIM, 2, dtype=F32) / HEAD_DIM)
    ang = jnp.arange(L, dtype=F32)[:, None] * inv[None]
    cos, sin = jnp.cos(ang), jnp.sin(ang)
    cos_t = jnp.tile(jnp.concatenate([cos, cos], axis=-1), (1, N_HEADS))
    sin_t = jnp.tile(jnp.concatenate([-sin, sin], axis=-1), (1, N_HEADS))
    return cos_t, sin_t


def _lanes(v):
    return jnp.repeat(v, HEAD_DIM, axis=-1)[..., None, :]


def _twice_diag(w):
    z = jnp.zeros_like(w)
    return jnp.concatenate([jnp.concatenate([w, z], axis=-1), jnp.concatenate([z, w], axis=-1)], axis=-2)


def kernel(x, p, norm_mix_w, w_in, ret_gn_w, ssm_conv_w, ssm_conv_b, ssm_a_log, ssm_dt_bias, ssm_d, ssm_norm_w, hy_conv_w, hy_conv_b, hy_filt_w_in, hy_filt_b_in, hy_filt_w_mid, hy_filt_b_mid, hy_filt_freq, hy_filt_w_out, hy_bias, hy_norm_w, gdn_conv_w, gdn_a_log, gdn_dt_bias, gdn_norm_w, w_out, norm_ffn_w, router_w, exp_w_gate, exp_w_up, exp_w_down, ple_norm_w, ple_gate_w, ple_proj_w, final_norm_w):
    batch, L, _ = x.shape
    depth = w_in.shape[0]
    assert batch == 1 and L % TOKEN_BLOCK == 0 and (2 * L) % (8 * FFT_N2) == 0
    cap = CAPACITY_FACTOR * L // N_EXPERTS
    assert cap % ROW_TILE == 0
    n1 = 2 * L // FFT_N2
    nblk = L // TOKEN_BLOCK

    dmask, pwh, pwt, dec = (jnp.asarray(a) for a in _retention_tables())
    f1, f3, g_mat, gi_mat, twc, tws = (jnp.asarray(a) for a in _dft_tables(n1))
    exp_dt, exp_a, exp_b = (jnp.asarray(_expand_table(b)) for b in (0, 8, 16))
    cos_t, sin_t = _rope_tables(L)
    z_ext = _hyena_positions(L)
    max_decay = math.log(HY_DECAY_TARGET) / HY_FAST_PCT
    min_decay = math.log(HY_DECAY_TARGET) / HY_SLOW_PCT
    abs_delta = jnp.abs(jnp.linspace(min_decay, max_decay, GROUP_W, dtype=F32))[None, :]
    row = lambda v: v[None, :]
    twice = lambda v: jnp.concatenate([v, v], axis=-1)

    h = x[0]
    for i in range(depth):
        w = w_in[i]
        o_ssm = 4 * GROUP_W
        o_hy = o_ssm + GROUP_W + XBC_W + 2 * N_HEADS
        o_gdn = o_hy + 3 * GROUP_W
        o_ab = o_gdn + 4 * GROUP_W
        w_ret = w[:, 0:o_ssm].astype(BF16)
        w_ssm = jnp.concatenate([w[:, o_ssm + GROUP_W:o_ssm + GROUP_W + XBC_W], w[:, o_ssm:o_ssm + GROUP_W]],
                                axis=1).astype(BF16)
        w_hy = w[:, o_hy:o_gdn].astype(BF16)
        w_gdn = w[:, o_gdn:o_ab].astype(BF16)
        w_small = jnp.concatenate([w[:, o_hy - 2 * N_HEADS:o_hy], w[:, o_ab:o_ab + 4 * N_HEADS],
                                   jnp.zeros((D_MODEL, 128 - 6 * N_HEADS), F32)], axis=1).astype(BF16)
        c_ret, c_ssm, c_hy, c_gdn, c_small = _inproj(h, row(norm_mix_w[i]), w_ret, w_ssm, w_hy, w_gdn, w_small)

        ret_f, ret_b = (_retention(c_ret, cos_t, sin_t, dmask, pwh, pwt, dec, reverse=bool(d)) for d in range(2))

        ssm_dtb = _lanes(ssm_dt_bias[i])
        ssm_a = _lanes(-jnp.exp(ssm_a_log[i]))
        ssd_f, ssd_b = (_ssd(c_ssm, c_small, ssm_conv_w[i], row(ssm_conv_b[i]), exp_dt[d], ssm_dtb[d], ssm_a[d],
                             _lanes(ssm_d[i]), reverse=bool(d)) for d in range(2))

        gdn_dtb = _lanes(gdn_dt_bias[i])
        gdn_a = _lanes(-jnp.exp(gdn_a_log[i]))
        gdn_f, gdn_b = (_gdn(c_gdn, c_small, gdn_conv_w[i], exp_a[d], exp_b[d], gdn_dtb[d], gdn_a[d],
                             reverse=bool(d)) for d in range(2))

        hy_vx, hy_x0 = _hy_pre(c_hy, hy_conv_w[i], row(hy_conv_b[i]))
        w_in_pad = jnp.concatenate([hy_filt_w_in[i], jnp.zeros((HY_EMB_PAD - HY_EMB, HY_ORDER), F32)], axis=0)
        filt = _hy_filter(z_ext, _twice_diag(w_in_pad), row(twice(hy_filt_b_in[i])), _twice_diag(hy_filt_w_mid[i]),
                          twice(hy_filt_b_mid[i])[:, None, :], row(twice(hy_filt_freq[i])), hy_filt_w_out[i],
                          abs_delta)
        nk1 = f1.shape[0] // 2
        spec_shape = (2, nk1, FFT_N2, GROUP_W)
        f_a = _left_matmul(f1, filt.reshape(n1, FFT_N2 * GROUP_W), BF16).reshape(spec_shape)
        f_spec = _hy_filter_spec(f_a, twc, tws, g_mat)
        v_a = _left_matmul(f1[:, :n1 // 2], hy_vx.reshape(n1 // 2, FFT_N2 * GROUP_W), BF16).reshape(spec_shape)
        e_spec = _hy_conv_spec(v_a, twc, tws, g_mat, gi_mat, f_spec)
        hy_y = _left_matmul(f3, e_spec.reshape(2 * nk1, FFT_N2 * GROUP_W), F32).reshape(L, GROUP_W)

        h1, xn, aff = _postmix(h, ret_f, ret_b, c_ret, ssd_f, ssd_b, c_ssm, hy_y, hy_vx, hy_x0, gdn_f, gdn_b, c_gdn,
                               row(ret_gn_w[i]), row(ssm_norm_w[i]), row(hy_bias[i]), row(hy_norm_w[i]),
                               row(jnp.tile(gdn_norm_w[i], N_HEADS)), w_out[i].astype(BF16),
                               row(norm_ffn_w[i]), router_w[i])

        aff_t = aff.T
        sel = _topk(aff_t, cap)
        posm, offs = _positions(sel.reshape(N_EXPERTS * (L // 128), 128), N_EXPERTS)
        posm4 = posm.reshape(N_EXPERTS, nblk, 1, TOKEN_BLOCK)
        aff4 = aff_t.reshape(N_EXPERTS, nblk, 1, TOKEN_BLOCK)
        gather_work, scatter_work = _work_lists(offs[:, 0].reshape(N_EXPERTS, L // 128), cap, TOKEN_BLOCK)
        xs, gate_c = _gather(gather_work, posm4, aff4, xn, cap)
        moe = [_scatter(scatter_work[grp], posm4,
                        _expert_ffn(xs, gate_c, exp_w_gate, exp_w_up, exp_w_down, i, grp), grp)
               for grp in range(EXPERT_GROUPS)]

        h = _ple(h1, moe[0], moe[1], p, i, row(ple_norm_w[i]), ple_gate_w[i].astype(BF16), ple_proj_w[i].astype(BF16),
                 row(final_norm_w), final=(i == depth - 1))
    return h[None]
```

```python
import functools
import math

import numpy as np
import jax
import jax.numpy as jnp
from jax import lax
from jax.experimental import pallas as pl
from jax.experimental.pallas import tpu as pltpu

F32 = jnp.float32
BF16 = jnp.bfloat16
I32 = jnp.int32
HIGHEST = lax.Precision.HIGHEST

D_MODEL = 1024
GROUP_W = 256
HEAD_DIM = 64
N_HEADS = GROUP_W // HEAD_DIM
CHUNK = 128
SCAN_CHUNKS = 4
RET_CHUNKS = 4
EPS = 1e-6
ROPE_BASE = 10000.0
SSM_STATE = 128
SSM_CONV = 5
XBC_W = 3 * GROUP_W
HY_CONV = 3
HY_EMB = 33
HY_EMB_PAD = 64
HY_BANDS = (HY_EMB - 1) // 2
HY_ORDER = 64
HY_INNER = 2
HY_DECAY_TARGET = 1e-2
HY_FAST_PCT = 0.3
HY_SLOW_PCT = 1.5
HY_SHIFT = 0.05
GDN_CONV = 5
N_EXPERTS = 16
CAPACITY_FACTOR = 2
EXPERT_FF = 1024
PLE_DIM = 256

HALO = 8
FFT_N2 = 256
FFT_K1_STEP = 8
ROW_TILE = 128
ROW_ALIGN = 16
TOKEN_BLOCK = 512
EXPERT_GROUPS = 2
VMEM_LIMIT = 48 * 1024 * 1024
VMEM_LIMIT_RESIDENT = 58 * 1024 * 1024


def _dot(a, b):
    return jnp.dot(a.astype(BF16), b.astype(BF16), preferred_element_type=F32)


def _dot_hi(a, b):
    return jnp.dot(a, b, precision=HIGHEST, preferred_element_type=F32)


def _dot_3pass(a, b):
    ah = a.astype(BF16)
    bh = b.astype(BF16)
    al = (a - ah.astype(F32)).astype(BF16)
    bl = (b - bh.astype(F32)).astype(BF16)
    dot = functools.partial(jnp.dot, preferred_element_type=F32)
    return dot(ah, bh) + (dot(ah, bl) + dot(al, bh))


def _dot_split(a, b, terms):
    bb = b.astype(BF16)
    out, rest = None, a
    for _ in range(terms):
        piece = rest.astype(BF16)
        part = jnp.dot(piece, bb, preferred_element_type=F32)
        out = part if out is None else out + part
        rest = rest - piece.astype(F32)
    return out


def _dot_split_rhs(a, b, terms):
    ab = a.astype(BF16)
    out, rest = None, b
    for _ in range(terms):
        piece = rest.astype(BF16)
        part = jnp.dot(ab, piece, preferred_element_type=F32)
        out = part if out is None else out + part
        rest = rest - piece.astype(F32)
    return out


def _dot_nt(a, b):
    return lax.dot_general(a.astype(BF16), b.astype(BF16), (((1,), (1,)), ((), ())),
                           preferred_element_type=F32)


def _dot_tn(a, b):
    return lax.dot_general(a.astype(BF16), b.astype(BF16), (((0,), (0,)), ((), ())),
                           preferred_element_type=F32)


def _silu(x):
    return x * (1.0 / (1.0 + jnp.exp(-x)))


def _sigmoid(x):
    return 1.0 / (1.0 + jnp.exp(-x))


def _softplus(x):
    return jnp.maximum(x, 0.0) + jnp.log(1.0 + jnp.exp(-jnp.abs(x)))


def _head_masks(rows):
    lane = lax.broadcasted_iota(I32, (rows, GROUP_W), 1)
    return [(lane >= h * HEAD_DIM) & (lane < (h + 1) * HEAD_DIM) for h in range(N_HEADS)]


def _stack_heads(x, hms):
    return jnp.concatenate([jnp.where(hm, x, 0.0) for hm in hms], axis=0)


def _pair_diag(x):
    t = x.shape[0]
    left = lax.broadcasted_iota(I32, x.shape, 1) < t
    return jnp.concatenate([jnp.where(left, x, 0.0), jnp.where(left, 0.0, x)], axis=0)


def _block_diag_mask():
    r = lax.broadcasted_iota(I32, (GROUP_W, GROUP_W), 0) // HEAD_DIM
    c = lax.broadcasted_iota(I32, (GROUP_W, GROUP_W), 1) // HEAD_DIM
    return r == c


def _params(sem, vmem=VMEM_LIMIT):
    return pltpu.CompilerParams(dimension_semantics=sem, vmem_limit_bytes=vmem)


def _conv_halo(prev_ref, main_ref, next_ref, w_ref, ext_ref, is_first, is_last, width):
    rows = main_ref.shape[0]
    ext_ref[0:HALO, :] = jnp.where(is_first, 0.0, prev_ref[...])
    ext_ref[HALO:HALO + rows, :] = main_ref[...]
    ext_ref[HALO + rows:2 * HALO + rows, :] = jnp.where(is_last, 0.0, next_ref[...])
    pad = (width - 1) // 2
    acc = None
    for k in range(width):
        term = ext_ref[pl.ds(HALO - pad + k, rows), :] * w_ref[k:k + 1, :]
        acc = term if acc is None else acc + term
    return acc


def _halo_specs(rows, width, nblocks, reverse):
    per = rows // HALO
    last8 = nblocks * per - 1
    blk = (lambda i: nblocks - 1 - i) if reverse else (lambda i: i)
    prev = pl.BlockSpec((HALO, width), lambda i: (jnp.maximum(blk(i) * per - 1, 0), 0))
    main = pl.BlockSpec((rows, width), lambda i: (blk(i), 0))
    nxt = pl.BlockSpec((HALO, width), lambda i: (jnp.minimum((blk(i) + 1) * per, last8), 0))
    return prev, main, nxt, blk


def _scan_masks(reverse):
    ri = lax.broadcasted_iota(I32, (CHUNK, CHUNK), 0)
    ci = lax.broadcasted_iota(I32, (CHUNK, CHUNK), 1)
    vis = (ci >= ri) if reverse else (ci <= ri)
    return ri, ci, vis, jnp.where(vis, 1.0, 0.0)


def _inproj_kernel(h_ref, nw_ref, wr_ref, ws_ref, wh_ref, wg_ref, wm_ref,
                   o_ret, o_ssm, o_hy, o_gdn, o_small):
    x = h_ref[...]
    y = x * lax.rsqrt(jnp.mean(x * x, axis=-1, keepdims=True) + EPS) * nw_ref[...]
    yb = y.astype(BF16)
    o_ret[...] = jnp.dot(yb, wr_ref[...], preferred_element_type=F32)
    o_ssm[...] = jnp.dot(yb, ws_ref[...], preferred_element_type=F32)
    o_hy[...] = jnp.dot(yb, wh_ref[...], preferred_element_type=F32)
    o_gdn[...] = jnp.dot(yb, wg_ref[...], preferred_element_type=F32)
    o_small[...] = jnp.dot(yb, wm_ref[...], preferred_element_type=F32)


def _inproj(h, norm_w, w_ret, w_ssm, w_hy, w_gdn, w_small, tm=512):
    L = h.shape[0]
    tm = min(tm, L)
    widths = [w.shape[1] for w in (w_ret, w_ssm, w_hy, w_gdn, w_small)]
    full = lambda w: pl.BlockSpec(w.shape, lambda i: (0, 0))
    return pl.pallas_call(
        _inproj_kernel,
        grid=(L // tm,),
        in_specs=[pl.BlockSpec((tm, D_MODEL), lambda i: (i, 0)), full(norm_w),
                  full(w_ret), full(w_ssm), full(w_hy), full(w_gdn), full(w_small)],
        out_specs=[pl.BlockSpec((tm, n), lambda i: (i, 0)) for n in widths],
        out_shape=[jax.ShapeDtypeStruct((L, n), F32) for n in widths],
        compiler_params=_params(("parallel",)),
        name="inproj",
    )(h, norm_w, w_ret, w_ssm, w_hy, w_gdn, w_small)


def _rope(x, cos, sin_signed, first_half):
    half = HEAD_DIM // 2
    swapped = jnp.where(first_half, pltpu.roll(x, GROUP_W - half, 1), pltpu.roll(x, half, 1))
    return x * cos + swapped * sin_signed


def _ret_kernel(cols_ref, cos_ref, sin_ref, dmask_ref, pwh_ref, pwt_ref, dec_ref, o_ref, s_ref, *, reverse):
    i = pl.program_id(0)
    T = CHUNK

    @pl.when(i == 0)
    def _():
        s_ref[...] = jnp.zeros_like(s_ref)

    rows = cols_ref.shape[0]
    lane = lax.broadcasted_iota(I32, (rows, GROUP_W), 1)
    first_half = (lane % HEAD_DIM) < (HEAD_DIM // 2)
    cos = cos_ref[...]
    sin = sin_ref[...]
    q = _rope(cols_ref[:, 0:GROUP_W], cos, sin, first_half)
    k = _rope(cols_ref[:, GROUP_W:2 * GROUP_W], cos, sin, first_half) * (HEAD_DIM ** -0.5)
    v = cols_ref[:, 2 * GROUP_W:3 * GROUP_W]
    pw_q = pwt_ref[...] if reverse else pwh_ref[...]
    pw_k = pwh_ref[...] if reverse else pwt_ref[...]
    bd = _block_diag_mask()
    hms = _head_masks(T)
    n_sub = rows // T
    intra = []
    if not reverse:
        for s in range(n_sub):
            q_s, k_s, v_s = (a[s * T:(s + 1) * T] for a in (q, k, v))
            s_all = _dot_nt(_stack_heads(q_s, hms), k_s)
            s_cat = jnp.concatenate([s_all[h * T:(h + 1) * T] * dmask_ref[h] for h in range(N_HEADS)], axis=1)
            intra.append(_dot(s_cat, _stack_heads(v_s, hms)))
    state = s_ref[...]
    for s in (reversed(range(n_sub)) if reverse else range(n_sub)):
        q_s, k_s, v_s = (a[s * T:(s + 1) * T] for a in (q, k, v))
        o = _dot(q_s * pw_q, state)
        o_ref[s * T:(s + 1) * T, :] = o if reverse else o + intra[s]
        state = dec_ref[...] * state + jnp.where(bd, _dot_tn(k_s * pw_k, v_s), 0.0)
    s_ref[...] = state


def _retention(cols_ret, cos_t, sin_t, dmask, pwh, pwt, dec, reverse):
    L = cols_ret.shape[0]
    rows = RET_CHUNKS * CHUNK
    nblk = L // rows
    blk = (lambda i: (nblk - 1 - i, 0)) if reverse else (lambda i: (i, 0))
    const = lambda a: pl.BlockSpec(a.shape, lambda i: (0,) * a.ndim)
    return pl.pallas_call(
        functools.partial(_ret_kernel, reverse=reverse),
        grid=(nblk,),
        in_specs=[pl.BlockSpec((rows, 4 * GROUP_W), blk),
                  pl.BlockSpec((rows, GROUP_W), blk), pl.BlockSpec((rows, GROUP_W), blk),
                  const(dmask), const(pwh), const(pwt), const(dec)],
        out_specs=pl.BlockSpec((rows, GROUP_W), blk),
        out_shape=jax.ShapeDtypeStruct((L, GROUP_W), F32),
        scratch_shapes=[pltpu.VMEM((GROUP_W, GROUP_W), F32)],
        compiler_params=_params(("arbitrary",)),
        name="retention_bwd" if reverse else "retention_fwd",
    )(cols_ret, cos_t, sin_t, dmask, pwh, pwt, dec)


def _ssd_kernel(xbc_ref, small_ref, exp_ref, dtb_ref, a_ref, dskip_ref, o_ref, s_ref, *, reverse):
    i = pl.program_id(0)
    T = CHUNK

    @pl.when(i == 0)
    def _():
        s_ref[...] = jnp.zeros_like(s_ref)

    xbc = xbc_ref[...]
    xs = xbc[:, 0:GROUP_W]
    dt = _softplus(_dot_split(small_ref[...], exp_ref[...], 3) + dtb_ref[...])
    a = dt * a_ref[...]
    xdt = xs * dt
    _, _, vis, trif = _scan_masks(reverse)
    hms = _head_masks(T)
    prep = []
    for s in range(SCAN_CHUNKS):
        r0, r1 = s * T, (s + 1) * T
        a_cum = _dot_split_rhs(trif, a[r0:r1], 3)
        a_tot = jnp.sum(a[r0:r1], axis=0, keepdims=True)
        xdt_s = xdt[r0:r1]
        scores, c_exp, b_dec = [], [], []
        for g in range(2):
            bm = xbc[r0:r1, GROUP_W + g * SSM_STATE:GROUP_W + (g + 1) * SSM_STATE]
            cm = xbc[r0:r1, 2 * GROUP_W + g * SSM_STATE:2 * GROUP_W + (g + 1) * SSM_STATE]
            cb = _dot_nt(cm, bm)
            for h in (2 * g, 2 * g + 1):
                col = jnp.broadcast_to(a_cum[:, h * HEAD_DIM:h * HEAD_DIM + 1], (T, T))
                tot = jnp.broadcast_to(a_tot[:, h * HEAD_DIM:h * HEAD_DIM + 1], (T, T))
                scores.append(cb * jnp.where(vis, jnp.exp(col - col.T), 0.0))
                c_exp.append(cm * jnp.exp(col))
                b_dec.append(bm * jnp.exp(tot - col))
        y_in = _dot(jnp.concatenate(scores, axis=1), _stack_heads(xdt_s, hms))
        prep.append((y_in, jnp.concatenate(c_exp, axis=1).astype(BF16), jnp.concatenate(b_dec, axis=1).astype(BF16),
                     xdt_s.astype(BF16), jnp.exp(a_tot)))
    state = s_ref[...]
    for s in (reversed(range(SCAN_CHUNKS)) if reverse else range(SCAN_CHUNKS)):
        y, c_cat, b_cat, xdt_s, decay = prep[s]
        y = y + _dot(c_cat, _stack_heads(state, hms))
        upd_all = _dot_tn(b_cat, xdt_s)
        upd = jnp.zeros((SSM_STATE, GROUP_W), F32)
        for h in range(N_HEADS):
            upd = upd + jnp.where(hms[h], upd_all[h * SSM_STATE:(h + 1) * SSM_STATE], 0.0)
        state = decay * state + upd
        if not reverse:
            y = y + xs[s * T:(s + 1) * T] * dskip_ref[...]
        o_ref[s * T:(s + 1) * T, :] = y
    s_ref[...] = state


def _scan_specs(L, reverse):
    rows = SCAN_CHUNKS * CHUNK
    nblk = L // rows
    blk = (lambda i: (nblk - 1 - i, 0)) if reverse else (lambda i: (i, 0))
    return nblk, (lambda width: pl.BlockSpec((rows, width), blk))


def _ssd(xbc_act, cols_small, expand, dt_bias, a_neg, d_skip, reverse):
    L = xbc_act.shape[0]
    nblk, rows_of = _scan_specs(L, reverse)
    const = lambda a: pl.BlockSpec(a.shape, lambda i: (0,) * a.ndim)
    return pl.pallas_call(
        functools.partial(_ssd_kernel, reverse=reverse),
        grid=(nblk,),
        in_specs=[rows_of(XBC_W), rows_of(128), const(expand), const(dt_bias), const(a_neg), const(d_skip)],
        out_specs=rows_of(GROUP_W),
        out_shape=jax.ShapeDtypeStruct((L, GROUP_W), F32),
        scratch_shapes=[pltpu.VMEM((SSM_STATE, GROUP_W), F32)],
        compiler_params=_params(("arbitrary",)),
        name="ssd_bwd" if reverse else "ssd_fwd",
    )(xbc_act, cols_small, expand, dt_bias, a_neg, d_skip)


def _unit_tri_inverse_pairs(a_list, eye2, bd16, offs):
    bdot = functools.partial(jnp.dot, preferred_element_type=F32)
    ab = [a.astype(BF16) for a in a_list]
    n1 = [jnp.where(bd16, -a, 0.0) for a in a_list]
    t = [eye2 + x for x in n1]
    pb = [jnp.where(bd16, -a, 0.0) for a in ab]
    p = [bdot(x, _pair_diag(x)) for x in pb]
    for lvl in range(3):
        pb = [x.astype(BF16) for x in p]
        pd = [_pair_diag(x) for x in pb]
        t = [x + bdot(x.astype(BF16), d) for x, d in zip(t, pd)]
        if lvl < 2:
            p = [bdot(x, d) for x, d in zip(pb, pd)]
    for off in offs:
        tb = [x.astype(BF16) for x in t]
        m = [bdot(x, _pair_diag(jnp.where(off, a, 0.0))) for x, a in zip(tb, ab)]
        t = [x - bdot(y.astype(BF16), _pair_diag(xb)) for x, y, xb in zip(t, m, tb)]
    return t


def _gdn_kernel(qkv_ref, small_ref, expa_ref, expb_ref, dtb_ref, a_ref, o_ref, s_ref, *, reverse):
    i = pl.program_id(0)
    T = CHUNK

    @pl.when(i == 0)
    def _():
        s_ref[...] = jnp.zeros_like(s_ref)

    bdf = jnp.where(_block_diag_mask(), 1.0, 0.0)
    q = qkv_ref[:, 0:GROUP_W]
    k = qkv_ref[:, GROUP_W:2 * GROUP_W]
    v = qkv_ref[:, 2 * GROUP_W:3 * GROUP_W]
    small = small_ref[...]
    g = a_ref[...] * _softplus(_dot_split(small, expa_ref[...], 3) + dtb_ref[...])
    beta = _sigmoid(_dot_split(small, expb_ref[...], 3))
    ri, ci, vis, trif = _scan_masks(reverse)
    strict = vis & (ri != ci)
    r2 = lax.broadcasted_iota(I32, (T, 2 * T), 0)
    c2 = lax.broadcasted_iota(I32, (T, 2 * T), 1) % T
    eye2 = jnp.where(r2 == c2, 1.0, 0.0)
    bd16 = (r2 // 16) == (c2 // 16)
    offs = []
    for sz in (16, 32, 64):
        pair = (r2 // (2 * sz)) == (c2 // (2 * sz))
        later, earlier = (c2, r2) if reverse else (r2, c2)
        offs.append(pair & ((later // sz) % 2 == 1) & ((earlier // sz) % 2 == 0))
    kb = k * beta
    vb = v * beta
    hms = _head_masks(T)
    pre, a_pairs = [], []
    for s in range(SCAN_CHUNKS):
        r0, r1 = s * T, (s + 1) * T
        gc = _dot_split_rhs(trif, g[r0:r1], 3)
        g_tot = jnp.sum(g[r0:r1], axis=0, keepdims=True)
        egc = jnp.exp(gc)
        k_s, q_s = k[r0:r1], q[r0:r1]
        a_all = _dot_nt(_stack_heads(kb[r0:r1], hms), k_s)
        q_all = _dot_nt(_stack_heads(q_s, hms), k_s)
        a_h, qk_h = [], []
        for h in range(N_HEADS):
            col = jnp.broadcast_to(gc[:, h * HEAD_DIM:h * HEAD_DIM + 1], (T, T))
            decay = jnp.where(vis, jnp.exp(col - col.T), 0.0)
            a_h.append(jnp.where(strict, a_all[h * T:(h + 1) * T] * decay, 0.0))
            qk_h.append(q_all[h * T:(h + 1) * T] * decay)
        for pr in range(N_HEADS // 2):
            a_pairs.append(jnp.concatenate(a_h[2 * pr:2 * pr + 2], axis=1))
        rhs = jnp.concatenate([_stack_heads(vb[r0:r1], hms), _stack_heads(kb[r0:r1] * egc, hms)], axis=1)
        pre.append((jnp.concatenate(qk_h, axis=1).astype(BF16), rhs.astype(BF16), (q_s * egc).astype(BF16),
                    (k_s * jnp.exp(g_tot - gc)).astype(BF16), jnp.exp(g_tot)))
    t_pairs = _unit_tri_inverse_pairs(a_pairs, eye2, bd16, offs)
    npair = N_HEADS // 2
    uw = [_dot(jnp.concatenate(t_pairs[s * npair:(s + 1) * npair], axis=1), pre[s][1])
          for s in range(SCAN_CHUNKS)]
    state = s_ref[...]
    for s in (reversed(range(SCAN_CHUNKS)) if reverse else range(SCAN_CHUNKS)):
        qk_cat, _, q_dec, k_dec, decay = pre[s]
        v_new = uw[s][:, 0:GROUP_W] - _dot(uw[s][:, GROUP_W:2 * GROUP_W], state)
        o_ref[s * T:(s + 1) * T, :] = _dot(q_dec, state) + _dot(qk_cat, _stack_heads(v_new, hms))
        state = state * decay + bdf * _dot_tn(k_dec, v_new)
    s_ref[...] = state


def _gdn(qkv_act, cols_small, exp_a, exp_b, dt_bias, a_neg, reverse):
    L = qkv_act.shape[0]
    nblk, rows_of = _scan_specs(L, reverse)
    const = lambda a: pl.BlockSpec(a.shape, lambda i: (0,) * a.ndim)
    return pl.pallas_call(
        functools.partial(_gdn_kernel, reverse=reverse),
        grid=(nblk,),
        in_specs=[rows_of(XBC_W), rows_of(128), const(exp_a), const(exp_b), const(dt_bias), const(a_neg)],
        out_specs=rows_of(GROUP_W),
        out_shape=jax.ShapeDtypeStruct((L, GROUP_W), F32),
        scratch_shapes=[pltpu.VMEM((GROUP_W, GROUP_W), F32)],
        compiler_params=_params(("arbitrary",)),
        name="gdn_bwd" if reverse else "gdn_fwd",
    )(qkv_act, cols_small, exp_a, exp_b, dt_bias, a_neg)


def _mixer_act_kernel(sp_ref, sm_ref, sn_ref, gp_ref, gm_ref, gn_ref, scw_ref, scb_ref, gcw_ref,
                      so_ref, go_ref, sext_ref, gext_ref):
    i = pl.program_id(0)
    n = pl.num_programs(0)
    first, last = i == 0, i == n - 1
    so_ref[...] = _silu(_conv_halo(sp_ref, sm_ref, sn_ref, scw_ref, sext_ref, first, last, SSM_CONV) + scb_ref[...])
    qkv = _silu(_conv_halo(gp_ref, gm_ref, gn_ref, gcw_ref, gext_ref, first, last, GDN_CONV))
    bdf = jnp.where(_block_diag_mask(), 1.0, 0.0)
    q = qkv[:, 0:GROUP_W]
    k = qkv[:, GROUP_W:2 * GROUP_W]
    go_ref[:, 0:GROUP_W] = q * lax.rsqrt(_dot_split(q * q, bdf, 2) + EPS) * (HEAD_DIM ** -0.5)
    go_ref[:, GROUP_W:2 * GROUP_W] = k * lax.rsqrt(_dot_split(k * k, bdf, 2) + EPS)
    go_ref[:, 2 * GROUP_W:3 * GROUP_W] = qkv[:, 2 * GROUP_W:3 * GROUP_W]


def _mixer_act(cols_ssm, cols_gdn, ssm_cw, ssm_cb, gdn_cw, tm=512):
    L = cols_ssm.shape[0]
    tm = min(tm, L)
    prev, main, nxt, _ = _halo_specs(tm, XBC_W, L // tm, False)
    const = lambda a: pl.BlockSpec(a.shape, lambda i: (0,) * a.ndim)
    return pl.pallas_call(
        _mixer_act_kernel,
        grid=(L // tm,),
        in_specs=[prev, main, nxt, prev, main, nxt, const(ssm_cw), const(ssm_cb), const(gdn_cw)],
        out_specs=[pl.BlockSpec((tm, XBC_W), lambda i: (i, 0))] * 2,
        out_shape=[jax.ShapeDtypeStruct((L, XBC_W), F32)] * 2,
        scratch_shapes=[pltpu.VMEM((tm + 2 * HALO, XBC_W), F32)] * 2,
        compiler_params=_params(("parallel",)),
        name="mixer_act",
    )(cols_ssm, cols_ssm, cols_ssm, cols_gdn, cols_gdn, cols_gdn, ssm_cw, ssm_cb, gdn_cw)


def _hy_pre_kernel(prev_ref, main_ref, next_ref, cw_ref, cb_ref, vx_ref, x0_ref, vxb_ref, ext_ref):
    i = pl.program_id(0)
    n = pl.num_programs(0)
    u = _conv_halo(prev_ref, main_ref, next_ref, cw_ref, ext_ref, i == 0, i == n - 1, HY_CONV) + cb_ref[...]
    x0_ref[...] = u[:, 0:GROUP_W]
    vx = u[:, 2 * GROUP_W:3 * GROUP_W] * u[:, GROUP_W:2 * GROUP_W]
    vx_ref[...] = vx
    vxb_ref[...] = vx.astype(BF16)


def _hy_pre(cols_hy, conv_w, conv_b, tm=512):
    L = cols_hy.shape[0]
    tm = min(tm, L)
    w = 3 * GROUP_W
    prev, main, nxt, _ = _halo_specs(tm, w, L // tm, False)
    const = lambda a: pl.BlockSpec(a.shape, lambda i: (0,) * a.ndim)
    return pl.pallas_call(
        _hy_pre_kernel,
        grid=(L // tm,),
        in_specs=[prev, main, nxt, const(conv_w), const(conv_b)],
        out_specs=[pl.BlockSpec((tm, GROUP_W), lambda i: (i, 0))] * 3,
        out_shape=[jax.ShapeDtypeStruct((L, GROUP_W), F32)] * 2 + [jax.ShapeDtypeStruct((L, GROUP_W), BF16)],
        scratch_shapes=[pltpu.VMEM((tm + 2 * HALO, w), F32)],
        compiler_params=_params(("parallel",)),
        name="hyena_pre",
    )(cols_hy, cols_hy, cols_hy, conv_w, conv_b)


def _hy_filter_kernel(z_ref, win_ref, bin_ref, wmid_ref, bmid_ref, freq_ref, wout_ref, delta_ref,
                      hf_ref, hb_ref):
    i = pl.program_id(0)
    tm = z_ref.shape[0]
    half = tm // 2
    z = z_ref[...]
    zz = jnp.concatenate([z[0:half], z[half:tm]], axis=1)
    freq = freq_ref[...]
    h = jnp.sin(freq * (_dot_hi(zz, win_ref[...]) + bin_ref[...]))
    for j in range(HY_INNER):
        h = jnp.sin(freq * (_dot_hi(h, wmid_ref[j]) + bmid_ref[j]))
    row = i * tm + lax.broadcasted_iota(I32, (half, 1), 0)
    for part in range(2):
        out = _dot_hi(h[:, part * HY_ORDER:(part + 1) * HY_ORDER], wout_ref[...])
        t = z[part * half:(part + 1) * half, 0:1]
        window = jnp.exp(-t * delta_ref[...]) + HY_SHIFT
        rows = slice(part * half, (part + 1) * half)
        hf_ref[rows, :] = (out[:, 0:GROUP_W] * window).astype(hf_ref.dtype)
        hb_ref[rows, :] = jnp.where(row + part * half == 0, 0.0,
                                    out[:, GROUP_W:2 * GROUP_W] * window).astype(hb_ref.dtype)


def _hy_filter(z_pos, w_in2, b_in2, w_mid2, b_mid2, freq2, w_out, abs_delta, tm=512):
    L = z_pos.shape[0]
    tm = min(tm, L)
    const = lambda a: pl.BlockSpec(a.shape, lambda i: (0,) * a.ndim)
    return pl.pallas_call(
        _hy_filter_kernel,
        grid=(L // tm,),
        in_specs=[pl.BlockSpec((tm, HY_EMB_PAD), lambda i: (i, 0)), const(w_in2), const(b_in2),
                  const(w_mid2), const(b_mid2), const(freq2), const(w_out), const(abs_delta)],
        out_specs=[pl.BlockSpec((tm, GROUP_W), lambda i: (i, 0))] * 2,
        out_shape=[jax.ShapeDtypeStruct((L, GROUP_W), BF16)] * 2,
        compiler_params=_params(("parallel",)),
        name="hyena_filter",
    )(z_pos, w_in2, b_in2, w_mid2, b_mid2, freq2, w_out, abs_delta)


def _left_matmul_kernel(m_ref, x_ref, o_ref):
    o_ref[...] = _dot(m_ref[...], x_ref[...]).astype(o_ref.dtype)


def _left_matmul(m, x2d, out_dtype, cb=4096):
    rows, kdim = m.shape
    ncol = x2d.shape[1]
    cb = min(cb, ncol)
    return pl.pallas_call(
        _left_matmul_kernel,
        grid=(ncol // cb,),
        in_specs=[pl.BlockSpec((rows, kdim), lambda i: (0, 0)), pl.BlockSpec((kdim, cb), lambda i: (0, i))],
        out_specs=pl.BlockSpec((rows, cb), lambda i: (0, i)),
        out_shape=jax.ShapeDtypeStruct((rows, ncol), out_dtype),
        compiler_params=_params(("parallel",)),
        name="hyena_outer_dft",
    )(m, x2d)


def _twiddle_inner_dft(a_ref, twc_ref, tws_ref, g_ref, kk):
    ar = a_ref[0, kk].astype(F32)
    ai = a_ref[1, kk].astype(F32)
    cs = twc_ref[kk]
    sn = tws_ref[kk]
    br = ar * cs + ai * sn
    bi = ai * cs - ar * sn
    x = _dot(g_ref[...], jnp.concatenate([br, bi], axis=0))
    return x[0:FFT_N2], x[FFT_N2:2 * FFT_N2]


def _hy_filter_spec_kernel(af_ref, ab_ref, twc_ref, tws_ref, g_ref, x_ref):
    for kk in range(af_ref.shape[1]):
        fr, fi = _twiddle_inner_dft(af_ref, twc_ref, tws_ref, g_ref, kk)
        br, bi = _twiddle_inner_dft(ab_ref, twc_ref, tws_ref, g_ref, kk)
        x_ref[0, kk] = (fr + br).astype(x_ref.dtype)
        x_ref[1, kk] = (fi - bi).astype(x_ref.dtype)


def _hy_conv_spec_kernel(a_ref, twc_ref, tws_ref, g_ref, gi_ref, f_ref, e_ref):
    for kk in range(a_ref.shape[1]):
        xr, xi = _twiddle_inner_dft(a_ref, twc_ref, tws_ref, g_ref, kk)
        fr = f_ref[0, kk].astype(F32)
        fi = f_ref[1, kk].astype(F32)
        yr = xr * fr - xi * fi
        yi = xr * fi + xi * fr
        dd = _dot(gi_ref[...], jnp.concatenate([yr, yi], axis=0))
        dr = dd[0:FFT_N2]
        di = dd[FFT_N2:2 * FFT_N2]
        cs = twc_ref[kk]
        sn = tws_ref[kk]
        e_ref[0, kk] = (dr * cs - di * sn).astype(e_ref.dtype)
        e_ref[1, kk] = (dr * sn + di * cs).astype(e_ref.dtype)


def _hy_spec_specs():
    blk = pl.BlockSpec((2, FFT_K1_STEP, FFT_N2, GROUP_W), lambda k: (0, k, 0, 0))
    tw = pl.BlockSpec((FFT_K1_STEP, FFT_N2, 1), lambda k: (k, 0, 0))
    mat = pl.BlockSpec((2 * FFT_N2, 2 * FFT_N2), lambda k: (0, 0))
    return blk, tw, mat


def _hy_filter_spec(af4, ab4, twc, tws, g):
    blk, tw, mat = _hy_spec_specs()
    return pl.pallas_call(
        _hy_filter_spec_kernel,
        grid=(af4.shape[1] // FFT_K1_STEP,),
        in_specs=[blk, blk, tw, tw, mat],
        out_specs=blk,
        out_shape=jax.ShapeDtypeStruct(af4.shape, BF16),
        compiler_params=_params(("parallel",)),
        name="hyena_filter_spectrum",
    )(af4, ab4, twc, tws, g)


def _hy_conv_spec(a4, twc, tws, g, gi, fspec):
    blk, tw, mat = _hy_spec_specs()
    return pl.pallas_call(
        _hy_conv_spec_kernel,
        grid=(a4.shape[1] // FFT_K1_STEP,),
        in_specs=[blk, tw, tw, mat, mat, blk],
        out_specs=blk,
        out_shape=jax.ShapeDtypeStruct(a4.shape, BF16),
        compiler_params=_params(("parallel",)),
        name="hyena_spectral_product",
    )(a4, twc, tws, g, gi, fspec)


def _postmix_kernel(h_ref, retf_ref, retb_ref, g_ref, ssdf_ref, ssdb_ref, zs_ref, hy_ref, vx_ref, x0_ref,
                    gdnf_ref, gdnb_ref, zg_ref, gnw_ref, snw_ref, hyb_ref, hnw_ref, dnw_ref,
                    wout_ref, fnw_ref, rw_ref, h1_ref, xn_ref, aff_ref):
    avg = jnp.where(_block_diag_mask(), 1.0 / HEAD_DIM, 0.0)
    o = retf_ref[...] + retb_ref[...]
    mu = _dot_split(o, avg, 2)
    var = _dot_split(jnp.square(o - mu), avg, 2)
    m_ret = _silu(g_ref[...]) * ((o - mu) * lax.rsqrt(var + EPS) * gnw_ref[...])

    y = (ssdf_ref[...] + ssdb_ref[...]) * _silu(zs_ref[...])
    m_ssm = y * lax.rsqrt(jnp.mean(y * y, axis=-1, keepdims=True) + EPS) * snw_ref[...]

    t = (hy_ref[...] + vx_ref[...] * hyb_ref[...]) * x0_ref[...]
    m_hy = t * lax.rsqrt(jnp.mean(t * t, axis=-1, keepdims=True) + EPS) * hnw_ref[...]

    o = gdnf_ref[...] + gdnb_ref[...]
    m_gdn = o * lax.rsqrt(_dot_split(o * o, avg, 2) + EPS) * dnw_ref[...] * _silu(zg_ref[...])

    h1 = h_ref[...]
    for gi, m in enumerate((m_ret, m_ssm, m_hy, m_gdn)):
        h1 = h1 + jnp.dot(m.astype(BF16), wout_ref[gi * GROUP_W:(gi + 1) * GROUP_W, :],
                          preferred_element_type=F32)
    h1_ref[...] = h1
    xn = h1 * lax.rsqrt(jnp.mean(h1 * h1, axis=-1, keepdims=True) + EPS) * fnw_ref[...]
    xn_ref[...] = xn.astype(BF16)
    logits = _dot_3pass(xn, rw_ref[...])
    logits = logits - jnp.max(logits, axis=-1, keepdims=True)
    ex = jnp.exp(logits)
    aff_ref[...] = ex / jnp.sum(ex, axis=-1, keepdims=True)


def _postmix(h, ret_f, ret_b, cols_ret, ssd_f, ssd_b, cols_ssm, hy_y, hy_vx, hy_x0, gdn_f, gdn_b, cols_gdn,
             gn_w, ssm_nw, hy_bias, hy_nw, gdn_nw, w_out, ffn_nw, router_w, tm=256):
    L = h.shape[0]
    tm = min(tm, L)
    row = lambda n: pl.BlockSpec((tm, n), lambda i: (i, 0))
    grp = row(GROUP_W)
    zcol = pl.BlockSpec((tm, GROUP_W), lambda i: (i, 3))
    const = lambda a: pl.BlockSpec(a.shape, lambda i: (0,) * a.ndim)
    return pl.pallas_call(
        _postmix_kernel,
        grid=(L // tm,),
        in_specs=[row(D_MODEL), grp, grp, zcol, grp, grp, zcol, grp, grp, grp, grp, grp, zcol,
                  const(gn_w), const(ssm_nw), const(hy_bias), const(hy_nw), const(gdn_nw),
                  const(w_out), const(ffn_nw), const(router_w)],
        out_specs=[row(D_MODEL), row(D_MODEL), row(N_EXPERTS)],
        out_shape=[jax.ShapeDtypeStruct((L, D_MODEL), F32), jax.ShapeDtypeStruct((L, D_MODEL), BF16),
                   jax.ShapeDtypeStruct((L, N_EXPERTS), F32)],
        compiler_params=_params(("parallel",)),
        name="postmix",
    )(h, ret_f, ret_b, cols_ret, ssd_f, ssd_b, cols_ssm, hy_y, hy_vx, hy_x0, gdn_f, gdn_b, cols_gdn,
      gn_w, ssm_nw, hy_bias, hy_nw, gdn_nw, w_out, ffn_nw, router_w)


def _topk_kernel(aff_ref, sel_ref, *, cap, idx_bits):
    bits = lax.bitcast_convert_type(aff_ref[...], I32)
    shape = bits.shape
    capf = float(cap)

    def count(mask):
        return jnp.sum(jnp.where(mask, 1.0, 0.0), axis=1, keepdims=True)

    def value_step(i, thr):
        cand = thr | lax.shift_left(jnp.int32(1), 30 - i)
        return jnp.where(count(bits >= cand) >= capf, cand, thr)

    thr = lax.fori_loop(0, 31, value_step, jnp.zeros((shape[0], 1), I32))
    above = bits > thr
    tied = bits == thr
    need = capf - count(above)
    idx = lax.broadcasted_iota(I32, shape, 1)

    def index_step(i, m):
        cand = m | lax.shift_left(jnp.int32(1), idx_bits - 1 - i)
        return jnp.where(count(tied & (idx < cand)) < need, cand, m)

    last = lax.fori_loop(0, idx_bits, index_step, jnp.zeros((shape[0], 1), I32))
    sel_ref[...] = jnp.where(above | (tied & (idx <= last)), 1.0, 0.0)


def _topk(aff_t, cap):
    n_e, L = aff_t.shape
    return pl.pallas_call(
        functools.partial(_topk_kernel, cap=cap, idx_bits=int(math.log2(L)) + 1),
        out_shape=jax.ShapeDtypeStruct((n_e, L), F32),
        compiler_params=_params(None),
        name="expert_topk",
    )(aff_t)


def _positions_kernel(sel_ref, posm_ref, offs_ref, *, n_e):
    s = sel_ref[...]
    nb = s.shape[0] // n_e
    r = lax.broadcasted_iota(I32, (128, 128), 0)
    c = lax.broadcasted_iota(I32, (128, 128), 1)
    incl = _dot(s, jnp.where(r <= c, 1.0, 0.0))
    tot = _dot(s, jnp.ones((128, 128), F32))
    rb = lax.broadcasted_iota(I32, (nb, nb), 0)
    cbk = lax.broadcasted_iota(I32, (nb, nb), 1)
    before = jnp.where(cbk < rb, 1.0, 0.0)
    for e in range(n_e):
        offs = _dot(before, tot[e * nb:(e + 1) * nb])
        se = s[e * nb:(e + 1) * nb]
        pos = incl[e * nb:(e + 1) * nb] - se + offs
        posm_ref[e * nb:(e + 1) * nb, :] = jnp.where(se > 0.5, pos, -1.0).astype(I32)
        offs_ref[e * nb:(e + 1) * nb, :] = offs.astype(I32)


def _positions(sel2, n_e):
    return pl.pallas_call(
        functools.partial(_positions_kernel, n_e=n_e),
        out_shape=[jax.ShapeDtypeStruct(sel2.shape, I32)] * 2,
        compiler_params=_params(None),
        name="expert_positions",
    )(sel2)


def _window_hits(posm_row, lo, cap):
    base = pl.multiple_of(jnp.minimum(lo, cap - ROW_TILE), ROW_ALIGN)
    rows = base + lax.broadcasted_iota(I32, (ROW_TILE, posm_row.shape[1]), 0)
    return base, (posm_row == rows) & (rows >= lo)


def _gather_kernel(off_ref, lo_ref, b_ref, posm_ref, aff_ref, x_ref, o_ref, og_ref):
    e = pl.program_id(0)
    nblk, _, tb = posm_ref.shape
    cap = o_ref.shape[0]
    o_ref[...] = jnp.zeros_like(o_ref)
    og_ref[...] = jnp.zeros_like(og_ref)

    def select(i):
        b = jnp.minimum(b_ref[i], nblk - 1)
        base, hit = _window_hits(posm_ref[b], lo_ref[i], cap)
        x_blk = x_ref[pl.ds(pl.multiple_of(b * tb, tb), tb), :]
        rows = jnp.dot(jnp.where(hit, 1.0, 0.0).astype(BF16), x_blk, preferred_element_type=F32).astype(BF16)
        gate = jnp.sum(jnp.where(hit, aff_ref[b], 0.0), axis=1, keepdims=True)
        return base, rows, gate

    def item_pair(ip, carry):
        picked = [select(2 * ip), select(2 * ip + 1)]
        for base, rows, gate in picked:
            o_ref[pl.ds(base, ROW_TILE), :] += rows
            og_ref[pl.ds(base, ROW_TILE), :] += gate
        return carry

    lax.fori_loop(off_ref[e] // 2, off_ref[e + 1] // 2, item_pair, 0)


def _gather(work, posm4, aff4, xn, cap):
    n_e, nblk, _, tb = posm4.shape
    per_expert = pl.BlockSpec((None, nblk, 1, tb), lambda e, *_: (e, 0, 0, 0))
    return pl.pallas_call(
        _gather_kernel,
        grid_spec=pltpu.PrefetchScalarGridSpec(
            num_scalar_prefetch=3,
            grid=(n_e,),
            in_specs=[per_expert, per_expert, pl.BlockSpec(memory_space=pltpu.VMEM)],
            out_specs=[pl.BlockSpec((None, cap, D_MODEL), lambda e, *_: (e, 0, 0)),
                       pl.BlockSpec((None, cap, 1), lambda e, *_: (e, 0, 0))],
        ),
        out_shape=[jax.ShapeDtypeStruct((n_e, cap, D_MODEL), BF16),
                   jax.ShapeDtypeStruct((n_e, cap, 1), F32)],
        compiler_params=_params(("arbitrary",), VMEM_LIMIT_RESIDENT),
        name="expert_gather",
    )(*work, posm4, aff4, xn)


def _ffn_kernel(x_ref, gate_ref, wg_ref, wu_ref, wd_ref, o_ref, acc_ref):
    f = pl.program_id(1)

    @pl.when(f == 0)
    def _():
        acc_ref[...] = jnp.zeros_like(acc_ref)

    x = x_ref[...]
    gate = jnp.dot(x, wg_ref[...].astype(BF16), preferred_element_type=F32)
    up = jnp.dot(x, wu_ref[...].astype(BF16), preferred_element_type=F32)
    acc_ref[...] += _dot(_silu(gate) * up, wd_ref[...])

    @pl.when(f == pl.num_programs(1) - 1)
    def _():
        o_ref[...] = (acc_ref[...] * gate_ref[...]).astype(BF16)


def _expert_ffn(xs, gate_c, w_gate, w_up, w_down, layer, group, fb=512):
    n_e, cap, _ = xs.shape
    n_g = n_e // EXPERT_GROUPS
    e0 = group * n_g
    return pl.pallas_call(
        _ffn_kernel,
        grid=(n_g, EXPERT_FF // fb),
        in_specs=[pl.BlockSpec((None, cap, D_MODEL), lambda e, f: (e0 + e, 0, 0)),
                  pl.BlockSpec((None, cap, 1), lambda e, f: (e0 + e, 0, 0)),
                  pl.BlockSpec((None, None, D_MODEL, fb), lambda e, f: (layer, e0 + e, 0, f)),
                  pl.BlockSpec((None, None, D_MODEL, fb), lambda e, f: (layer, e0 + e, 0, f)),
                  pl.BlockSpec((None, None, fb, D_MODEL), lambda e, f: (layer, e0 + e, f, 0))],
        out_specs=pl.BlockSpec((None, cap, D_MODEL), lambda e, f: (e, 0, 0)),
        out_shape=jax.ShapeDtypeStruct((n_g, cap, D_MODEL), BF16),
        scratch_shapes=[pltpu.VMEM((cap, D_MODEL), F32)],
        compiler_params=_params(("parallel", "arbitrary"), VMEM_LIMIT_RESIDENT),
        name="expert_ffn",
    )(xs, gate_c, w_gate, w_up, w_down)


def _scatter_kernel(off_ref, lo_ref, e_ref, posm_ref, y_ref, o_ref):
    b = pl.program_id(0)
    n_g, cap, _ = y_ref.shape
    o_ref[...] = jnp.zeros_like(o_ref)

    def pick(i):
        e = jnp.minimum(e_ref[i], n_g - 1)
        base, hit = _window_hits(posm_ref[e], lo_ref[i], cap)
        return jnp.where(hit, 1.0, 0.0).astype(BF16), y_ref[e, pl.ds(base, ROW_TILE), :]

    def item_pair(ip, carry):
        p0, y0 = pick(2 * ip)
        p1, y1 = pick(2 * ip + 1)
        o_ref[...] += lax.dot_general(jnp.concatenate([p0, p1], axis=0), jnp.concatenate([y0, y1], axis=0),
                                      (((0,), (0,)), ((), ())), preferred_element_type=F32)
        return carry

    lax.fori_loop(off_ref[b] // 2, off_ref[b + 1] // 2, item_pair, 0)


def _scatter(work, posm4, ye, group):
    n_e, nblk, _, tb = posm4.shape
    n_g = ye.shape[0]
    return pl.pallas_call(
        _scatter_kernel,
        grid_spec=pltpu.PrefetchScalarGridSpec(
            num_scalar_prefetch=3,
            grid=(nblk,),
            in_specs=[pl.BlockSpec((n_g, None, 1, tb), lambda b, *_: (group, b, 0, 0)),
                      pl.BlockSpec(memory_space=pltpu.VMEM)],
            out_specs=pl.BlockSpec((tb, D_MODEL), lambda b, *_: (b, 0)),
        ),
        out_shape=jax.ShapeDtypeStruct((nblk * tb, D_MODEL), F32),
        compiler_params=_params(("arbitrary",), VMEM_LIMIT_RESIDENT),
        name="expert_scatter",
    )(*work, posm4, ye)


def _enumerate_items(n_items, first_row, cap, n_work):
    n_groups, n_members = n_items.shape
    parity = jnp.sum(n_items, axis=1, keepdims=True) % 2
    flat = jnp.concatenate([n_items, parity], axis=1).reshape(-1)
    first = jnp.concatenate([first_row, jnp.full((n_groups, 1), cap, I32)], axis=1).reshape(-1)
    ends = jnp.cumsum(flat)
    begins = ends - flat
    w = jnp.arange(n_work, dtype=I32)
    pair = jnp.sum((ends[None, :] <= w[:, None]).astype(I32), axis=1)
    onehot = pair[:, None] == jnp.arange(flat.shape[0], dtype=I32)[None, :]
    begin_w = jnp.sum(jnp.where(onehot, begins[None, :], 0), axis=1)
    first_w = jnp.sum(jnp.where(onehot, first[None, :], 0), axis=1)
    offsets = jnp.concatenate([jnp.zeros((1,), I32), ends.reshape(n_groups, n_members + 1)[:, -1]])
    lo = first_w + ROW_TILE * (w - begin_w)
    return offsets.astype(I32), lo.astype(I32), (pair % (n_members + 1)).astype(I32)


def _work_lists(offs, cap, tb):
    n_e, nb128 = offs.shape
    per = tb // 128
    nblk = nb128 // per
    start = offs[:, ::per]
    end = jnp.concatenate([start[:, 1:], jnp.full((n_e, 1), cap, I32)], axis=1)
    first_row = (start // ROW_ALIGN) * ROW_ALIGN
    n_items = jnp.where(end > start, (end - first_row + ROW_TILE - 1) // ROW_TILE, 0)
    per_expert = (cap + (ROW_TILE + ROW_ALIGN - 2) * nblk) // ROW_TILE + 2
    off_g, lo_g, b_g = _enumerate_items(n_items, first_row, cap, n_e * per_expert)
    n_g = n_e // EXPERT_GROUPS
    scatter_work = []
    for grp in range(EXPERT_GROUPS):
        sl = slice(grp * n_g, (grp + 1) * n_g)
        scatter_work.append(_enumerate_items(n_items[sl].T, first_row[sl].T, cap, n_g * per_expert + nblk))
    return (off_g, lo_g, b_g), scatter_work


def _ple_kernel(h_ref, moe_a_ref, moe_b_ref, p_ref, nw_ref, wg_ref, wp_ref, fw_ref, o_ref, *, final):
    h2 = h_ref[...] + (moe_a_ref[...] + moe_b_ref[...])
    hn = h2 * lax.rsqrt(jnp.mean(h2 * h2, axis=-1, keepdims=True) + EPS) * nw_ref[...]
    gate = _sigmoid(jnp.dot(hn.astype(BF16), wg_ref[...], preferred_element_type=F32))
    h3 = h2 + gate * jnp.dot(p_ref[...].astype(BF16), wp_ref[...], preferred_element_type=F32)
    if final:
        h3 = h3 * lax.rsqrt(jnp.mean(h3 * h3, axis=-1, keepdims=True) + EPS) * fw_ref[...]
    o_ref[...] = h3


def _ple(h1, moe_a, moe_b, p, layer, norm_w, w_gate, w_proj, final_w, final, tm=512):
    L = h1.shape[0]
    tm = min(tm, L)
    row = lambda n: pl.BlockSpec((tm, n), lambda i: (i, 0))
    const = lambda a: pl.BlockSpec(a.shape, lambda i: (0,) * a.ndim)
    return pl.pallas_call(
        functools.partial(_ple_kernel, final=final),
        grid=(L // tm,),
        in_specs=[row(D_MODEL), row(D_MODEL), row(D_MODEL),
                  pl.BlockSpec((None, None, tm, PLE_DIM), lambda i: (layer, 0, i, 0)),
                  const(norm_w), const(w_gate), const(w_proj), const(final_w)],
        out_specs=row(D_MODEL),
        out_shape=jax.ShapeDtypeStruct((L, D_MODEL), F32),
        compiler_params=_params(("parallel",)),
        name="ple",
    )(h1, moe_a, moe_b, p, norm_w, w_gate, w_proj, final_w)


def _expand_table(base):
    t = np.zeros((2, 128, GROUP_W), np.float32)
    for d in range(2):
        for h in range(N_HEADS):
            t[d, base + N_HEADS * d + h, h * HEAD_DIM:(h + 1) * HEAD_DIM] = 1.0
    return t


def _retention_tables():
    hh = np.arange(N_HEADS, dtype=np.float64)
    lg = np.log(1.0 - 2.0 ** (-5.0 - hh))
    t = np.arange(CHUNK, dtype=np.float64)
    dmask = np.exp(np.abs(t[:, None] - t[None, :])[None] * lg[:, None, None])
    lanes = np.repeat(lg, HEAD_DIM)[None, :]
    pwt = np.exp((CHUNK - t)[:, None] * lanes)
    pwh = np.exp(t[:, None] * lanes)
    dec = np.exp(CHUNK * lanes)
    return [a.astype(np.float32) for a in (dmask, pwh, pwt, dec)]


def _dft_tables(n1):
    n = n1 * FFT_N2
    nh = n1 // 2 + 1
    nhp = -(-nh // FFT_K1_STEP) * FFT_K1_STEP
    k1 = np.arange(nh, dtype=np.float64)
    t1 = np.arange(n1, dtype=np.float64)
    ang1 = 2.0 * np.pi * np.outer(k1, t1) / n1
    f1 = np.zeros((2 * nhp, n1))
    f1[0:nh] = np.cos(ang1)
    f1[nhp:nhp + nh] = -np.sin(ang1)
    weight = np.where((k1 == 0) | (k1 == n1 // 2), 1.0, 2.0)[None, :] / n
    f3 = np.zeros((n1 // 2, 2 * nhp))
    f3[:, 0:nh] = np.cos(ang1.T[:n1 // 2]) * weight
    f3[:, nhp:nhp + nh] = -np.sin(ang1.T[:n1 // 2]) * weight
    k2 = np.arange(FFT_N2, dtype=np.float64)
    ang2 = 2.0 * np.pi * np.outer(k2, k2) / FFT_N2
    c2, s2 = np.cos(ang2), np.sin(ang2)
    g = np.block([[c2, s2], [-s2, c2]])
    gi = np.block([[c2, -s2], [s2, c2]])
    angt = np.zeros((nhp, FFT_N2))
    angt[0:nh] = 2.0 * np.pi * np.outer(k1, k2) / n
    twc, tws = np.cos(angt)[:, :, None], np.sin(angt)[:, :, None]
    return [a.astype(np.float32) for a in (f1, f3, g, gi, twc, tws)]


def _hyena_positions(L):
    m = jnp.arange(L, dtype=F32)[:, None]
    t = m / (L - 1)
    bands = jnp.linspace(1e-4, HY_BANDS - 1, HY_BANDS, dtype=F32)
    ang = (2.0 * math.pi / L) * m * bands[None]
    return jnp.concatenate([t, jnp.cos(ang), -jnp.sin(ang), jnp.zeros((L, HY_EMB_PAD - HY_EMB), F32)], axis=-1)


def _rope_tables(L):
    inv = ROPE_BASE ** (-jnp.arange(0, HEAD_DIM, 2, dtype=F32) / HEAD_DIM)
    ang = jnp.arange(L, dtype=F32)[:, None] * inv[None]
    cos, sin = jnp.cos(ang), jnp.sin(ang)
    cos_t = jnp.tile(jnp.concatenate([cos, cos], axis=-1), (1, N_HEADS))
    sin_t = jnp.tile(jnp.concatenate([-sin, sin], axis=-1), (1, N_HEADS))
    return cos_t, sin_t


def _lanes(v):
    return jnp.repeat(v, HEAD_DIM, axis=-1)[..., None, :]


def _twice_diag(w):
    z = jnp.zeros_like(w)
    return jnp.concatenate([jnp.concatenate([w, z], axis=-1), jnp.concatenate([z, w], axis=-1)], axis=-2)


def kernel(x, p, norm_mix_w, w_in, ret_gn_w, ssm_conv_w, ssm_conv_b, ssm_a_log, ssm_dt_bias, ssm_d, ssm_norm_w, hy_conv_w, hy_conv_b, hy_filt_w_in, hy_filt_b_in, hy_filt_w_mid, hy_filt_b_mid, hy_filt_freq, hy_filt_w_out, hy_bias, hy_norm_w, gdn_conv_w, gdn_a_log, gdn_dt_bias, gdn_norm_w, w_out, norm_ffn_w, router_w, exp_w_gate, exp_w_up, exp_w_down, ple_norm_w, ple_gate_w, ple_proj_w, final_norm_w):
    batch, L, _ = x.shape
    depth = w_in.shape[0]
    assert batch == 1 and L % TOKEN_BLOCK == 0 and (2 * L) % (8 * FFT_N2) == 0
    cap = CAPACITY_FACTOR * L // N_EXPERTS
    assert cap % ROW_TILE == 0
    n1 = 2 * L // FFT_N2
    nblk = L // TOKEN_BLOCK

    dmask, pwh, pwt, dec = (jnp.asarray(a) for a in _retention_tables())
    f1, f3, g_mat, gi_mat, twc, tws = (jnp.asarray(a) for a in _dft_tables(n1))
    exp_dt, exp_a, exp_b = (jnp.asarray(_expand_table(b)) for b in (0, 8, 16))
    cos_t, sin_t = _rope_tables(L)
    z_pos = _hyena_positions(L)
    f1_half = f1[:, :n1 // 2]
    max_decay = math.log(HY_DECAY_TARGET) / HY_FAST_PCT
    min_decay = math.log(HY_DECAY_TARGET) / HY_SLOW_PCT
    abs_delta = jnp.abs(jnp.linspace(min_decay, max_decay, GROUP_W, dtype=F32))[None, :]
    row = lambda v: v[None, :]
    twice = lambda v: jnp.concatenate([v, v], axis=-1)

    h = x[0]
    for i in range(depth):
        w = w_in[i]
        o_ssm = 4 * GROUP_W
        o_hy = o_ssm + GROUP_W + XBC_W + 2 * N_HEADS
        o_gdn = o_hy + 3 * GROUP_W
        o_ab = o_gdn + 4 * GROUP_W
        w_ret = w[:, 0:o_ssm].astype(BF16)
        w_ssm = jnp.concatenate([w[:, o_ssm + GROUP_W:o_ssm + GROUP_W + XBC_W], w[:, o_ssm:o_ssm + GROUP_W]],
                                axis=1).astype(BF16)
        w_hy = w[:, o_hy:o_gdn].astype(BF16)
        w_gdn = w[:, o_gdn:o_ab].astype(BF16)
        w_small = jnp.concatenate([w[:, o_hy - 2 * N_HEADS:o_hy], w[:, o_ab:o_ab + 4 * N_HEADS],
                                   jnp.zeros((D_MODEL, 128 - 6 * N_HEADS), F32)], axis=1).astype(BF16)
        c_ret, c_ssm, c_hy, c_gdn, c_small = _inproj(h, row(norm_mix_w[i]), w_ret, w_ssm, w_hy, w_gdn, w_small)

        ret_f, ret_b = (_retention(c_ret, cos_t, sin_t, dmask, pwh, pwt, dec, reverse=bool(d)) for d in range(2))

        ssm_dtb = _lanes(ssm_dt_bias[i])
        ssm_a = _lanes(-jnp.exp(ssm_a_log[i]))
        xbc_act, qkv_act = _mixer_act(c_ssm, c_gdn, ssm_conv_w[i], row(ssm_conv_b[i]), gdn_conv_w[i])
        ssd_f, ssd_b = (_ssd(xbc_act, c_small, exp_dt[d], ssm_dtb[d], ssm_a[d], _lanes(ssm_d[i]), reverse=bool(d))
                        for d in range(2))

        gdn_dtb = _lanes(gdn_dt_bias[i])
        gdn_a = _lanes(-jnp.exp(gdn_a_log[i]))
        gdn_f, gdn_b = (_gdn(qkv_act, c_small, exp_a[d], exp_b[d], gdn_dtb[d], gdn_a[d], reverse=bool(d))
                        for d in range(2))

        hy_vx, hy_x0, hy_vxb = _hy_pre(c_hy, hy_conv_w[i], row(hy_conv_b[i]))
        w_in_pad = jnp.concatenate([hy_filt_w_in[i], jnp.zeros((HY_EMB_PAD - HY_EMB, HY_ORDER), F32)], axis=0)
        taps = _hy_filter(z_pos, _twice_diag(w_in_pad), row(twice(hy_filt_b_in[i])), _twice_diag(hy_filt_w_mid[i]),
                          twice(hy_filt_b_mid[i])[:, None, :], row(twice(hy_filt_freq[i])), hy_filt_w_out[i],
                          abs_delta)
        nk1 = f1.shape[0] // 2
        spec_shape = (2, nk1, FFT_N2, GROUP_W)
        outer_dft = lambda sig: _left_matmul(f1_half, sig.reshape(n1 // 2, FFT_N2 * GROUP_W), BF16).reshape(spec_shape)
        f_spec = _hy_filter_spec(outer_dft(taps[0]), outer_dft(taps[1]), twc, tws, g_mat)
        e_spec = _hy_conv_spec(outer_dft(hy_vxb), twc, tws, g_mat, gi_mat, f_spec)
        hy_y = _left_matmul(f3, e_spec.reshape(2 * nk1, FFT_N2 * GROUP_W), F32).reshape(L, GROUP_W)

        h1, xn, aff = _postmix(h, ret_f, ret_b, c_ret, ssd_f, ssd_b, c_ssm, hy_y, hy_vx, hy_x0, gdn_f, gdn_b, c_gdn,
                               row(ret_gn_w[i]), row(ssm_norm_w[i]), row(hy_bias[i]), row(hy_norm_w[i]),
                               row(jnp.tile(gdn_norm_w[i], N_HEADS)), w_out[i].astype(BF16),
                               row(norm_ffn_w[i]), router_w[i])

        aff_t = aff.T
        sel = _topk(aff_t, cap)
        posm, offs = _positions(sel.reshape(N_EXPERTS * (L // 128), 128), N_EXPERTS)
        posm4 = posm.reshape(N_EXPERTS, nblk, 1, TOKEN_BLOCK)
        aff4 = aff_t.reshape(N_EXPERTS, nblk, 1, TOKEN_BLOCK)
        gather_work, scatter_work = _work_lists(offs[:, 0].reshape(N_EXPERTS, L // 128), cap, TOKEN_BLOCK)
        xs, gate_c = _gather(gather_work, posm4, aff4, xn, cap)
        moe = [_scatter(scatter_work[grp], posm4,
                        _expert_ffn(xs, gate_c, exp_w_gate, exp_w_up, exp_w_down, i, grp), grp)
               for grp in range(EXPERT_GROUPS)]

        h = _ple(h1, moe[0], moe[1], p, i, row(ple_norm_w[i]), ple_gate_w[i].astype(BF16), ple_proj_w[i].astype(BF16),
                 row(final_norm_w), final=(i == depth - 1))
    return h[None]
```

```python
import functools
import math

import numpy as np
import jax
import jax.numpy as jnp
from jax import lax
from jax.experimental import pallas as pl
from jax.experimental.pallas import tpu as pltpu

F32 = jnp.float32
BF16 = jnp.bfloat16
I32 = jnp.int32
HIGHEST = lax.Precision.HIGHEST

D_MODEL = 1024
GROUP_W = 256
HEAD_DIM = 64
N_HEADS = GROUP_W // HEAD_DIM
CHUNK = 128
SCAN_CHUNKS = 4
RET_CHUNKS = 4
EPS = 1e-6
ROPE_BASE = 10000.0
SSM_STATE = 128
SSM_CONV = 5
XBC_W = 3 * GROUP_W
HY_CONV = 3
HY_EMB = 33
HY_EMB_PAD = 64
HY_BANDS = (HY_EMB - 1) // 2
HY_ORDER = 64
HY_INNER = 2
HY_DECAY_TARGET = 1e-2
HY_FAST_PCT = 0.3
HY_SLOW_PCT = 1.5
HY_SHIFT = 0.05
GDN_CONV = 5
N_EXPERTS = 16
CAPACITY_FACTOR = 2
EXPERT_FF = 1024
PLE_DIM = 256

HALO = 8
FFT_N2 = 256
FFT_K1_STEP = 8
ROW_TILE = 128
ROW_ALIGN = 16
TOKEN_BLOCK = 512
EXPERT_GROUPS = 2
VMEM_LIMIT = 48 * 1024 * 1024
VMEM_LIMIT_RESIDENT = 58 * 1024 * 1024


def _dot(a, b):
    return jnp.dot(a.astype(BF16), b.astype(BF16), preferred_element_type=F32)


def _dot_hi(a, b):
    return jnp.dot(a, b, precision=HIGHEST, preferred_element_type=F32)


def _dot_3pass(a, b):
    ah = a.astype(BF16)
    bh = b.astype(BF16)
    al = (a - ah.astype(F32)).astype(BF16)
    bl = (b - bh.astype(F32)).astype(BF16)
    dot = functools.partial(jnp.dot, preferred_element_type=F32)
    return dot(ah, bh) + (dot(ah, bl) + dot(al, bh))


def _dot_split(a, b, terms):
    bb = b.astype(BF16)
    out, rest = None, a
    for _ in range(terms):
        piece = rest.astype(BF16)
        part = jnp.dot(piece, bb, preferred_element_type=F32)
        out = part if out is None else out + part
        rest = rest - piece.astype(F32)
    return out


def _dot_split_rhs(a, b, terms):
    ab = a.astype(BF16)
    out, rest = None, b
    for _ in range(terms):
        piece = rest.astype(BF16)
        part = jnp.dot(ab, piece, preferred_element_type=F32)
        out = part if out is None else out + part
        rest = rest - piece.astype(F32)
    return out


def _dot_nt(a, b):
    return lax.dot_general(a.astype(BF16), b.astype(BF16), (((1,), (1,)), ((), ())),
                           preferred_element_type=F32)


def _dot_tn(a, b):
    return lax.dot_general(a.astype(BF16), b.astype(BF16), (((0,), (0,)), ((), ())),
                           preferred_element_type=F32)


def _silu(x):
    return x * (1.0 / (1.0 + jnp.exp(-x)))


def _sigmoid(x):
    return 1.0 / (1.0 + jnp.exp(-x))


def _softplus(x):
    return jnp.maximum(x, 0.0) + jnp.log(1.0 + jnp.exp(-jnp.abs(x)))


def _head_masks(rows):
    lane = lax.broadcasted_iota(I32, (rows, GROUP_W), 1)
    return [(lane >= h * HEAD_DIM) & (lane < (h + 1) * HEAD_DIM) for h in range(N_HEADS)]


def _stack_heads(x, hms):
    return jnp.concatenate([jnp.where(hm, x, 0.0) for hm in hms], axis=0)


def _pair_diag(x):
    t = x.shape[0]
    left = lax.broadcasted_iota(I32, x.shape, 1) < t
    return jnp.concatenate([jnp.where(left, x, 0.0), jnp.where(left, 0.0, x)], axis=0)


def _block_diag_mask():
    r = lax.broadcasted_iota(I32, (GROUP_W, GROUP_W), 0) // HEAD_DIM
    c = lax.broadcasted_iota(I32, (GROUP_W, GROUP_W), 1) // HEAD_DIM
    return r == c


def _params(sem, vmem=VMEM_LIMIT):
    return pltpu.CompilerParams(dimension_semantics=sem, vmem_limit_bytes=vmem)


def _conv_halo(prev_ref, main_ref, next_ref, w_ref, ext_ref, is_first, is_last, width):
    rows = main_ref.shape[0]
    ext_ref[0:HALO, :] = jnp.where(is_first, 0.0, prev_ref[...])
    ext_ref[HALO:HALO + rows, :] = main_ref[...]
    ext_ref[HALO + rows:2 * HALO + rows, :] = jnp.where(is_last, 0.0, next_ref[...])
    pad = (width - 1) // 2
    acc = None
    for k in range(width):
        term = ext_ref[pl.ds(HALO - pad + k, rows), :] * w_ref[k:k + 1, :]
        acc = term if acc is None else acc + term
    return acc


def _halo_specs(rows, width, nblocks, reverse):
    per = rows // HALO
    last8 = nblocks * per - 1
    blk = (lambda i: nblocks - 1 - i) if reverse else (lambda i: i)
    prev = pl.BlockSpec((HALO, width), lambda i: (jnp.maximum(blk(i) * per - 1, 0), 0))
    main = pl.BlockSpec((rows, width), lambda i: (blk(i), 0))
    nxt = pl.BlockSpec((HALO, width), lambda i: (jnp.minimum((blk(i) + 1) * per, last8), 0))
    return prev, main, nxt, blk


def _scan_masks(reverse):
    ri = lax.broadcasted_iota(I32, (CHUNK, CHUNK), 0)
    ci = lax.broadcasted_iota(I32, (CHUNK, CHUNK), 1)
    vis = (ci >= ri) if reverse else (ci <= ri)
    return ri, ci, vis, jnp.where(vis, 1.0, 0.0)


def _inproj_kernel(h_ref, nw_ref, wr_ref, ws_ref, wh_ref, wg_ref, wm_ref,
                   o_ret, o_ssm, o_hy, o_gdn, o_small):
    x = h_ref[...]
    y = x * lax.rsqrt(jnp.mean(x * x, axis=-1, keepdims=True) + EPS) * nw_ref[...]
    yb = y.astype(BF16)
    o_ret[...] = jnp.dot(yb, wr_ref[...], preferred_element_type=F32)
    o_ssm[...] = jnp.dot(yb, ws_ref[...], preferred_element_type=F32)
    o_hy[...] = jnp.dot(yb, wh_ref[...], preferred_element_type=F32)
    o_gdn[...] = jnp.dot(yb, wg_ref[...], preferred_element_type=F32)
    o_small[...] = jnp.dot(yb, wm_ref[...], preferred_element_type=F32)


def _inproj(h, norm_w, w_ret, w_ssm, w_hy, w_gdn, w_small, tm=512):
    L = h.shape[0]
    tm = min(tm, L)
    widths = [w.shape[1] for w in (w_ret, w_ssm, w_hy, w_gdn, w_small)]
    full = lambda w: pl.BlockSpec(w.shape, lambda i: (0, 0))
    return pl.pallas_call(
        _inproj_kernel,
        grid=(L // tm,),
        in_specs=[pl.BlockSpec((tm, D_MODEL), lambda i: (i, 0)), full(norm_w),
                  full(w_ret), full(w_ssm), full(w_hy), full(w_gdn), full(w_small)],
        out_specs=[pl.BlockSpec((tm, n), lambda i: (i, 0)) for n in widths],
        out_shape=[jax.ShapeDtypeStruct((L, n), F32) for n in widths],
        compiler_params=_params(("parallel",)),
        name="inproj",
    )(h, norm_w, w_ret, w_ssm, w_hy, w_gdn, w_small)


def _rope(x, cos, sin_signed, first_half):
    half = HEAD_DIM // 2
    swapped = jnp.where(first_half, pltpu.roll(x, GROUP_W - half, 1), pltpu.roll(x, half, 1))
    return x * cos + swapped * sin_signed


def _ret_kernel(cols_ref, cos_ref, sin_ref, dmask_ref, pwh_ref, pwt_ref, dec_ref, o_ref, s_ref, *, reverse):
    i = pl.program_id(0)
    T = CHUNK

    @pl.when(i == 0)
    def _():
        s_ref[...] = jnp.zeros_like(s_ref)

    rows = cols_ref.shape[0]
    lane = lax.broadcasted_iota(I32, (rows, GROUP_W), 1)
    first_half = (lane % HEAD_DIM) < (HEAD_DIM // 2)
    cos = cos_ref[...]
    sin = sin_ref[...]
    q = _rope(cols_ref[:, 0:GROUP_W], cos, sin, first_half)
    k = _rope(cols_ref[:, GROUP_W:2 * GROUP_W], cos, sin, first_half) * (HEAD_DIM ** -0.5)
    v = cols_ref[:, 2 * GROUP_W:3 * GROUP_W]
    pw_q = pwt_ref[...] if reverse else pwh_ref[...]
    pw_k = pwh_ref[...] if reverse else pwt_ref[...]
    bd = _block_diag_mask()
    hms = _head_masks(T)
    n_sub = rows // T
    intra = []
    if not reverse:
        for s in range(n_sub):
            q_s, k_s, v_s = (a[s * T:(s + 1) * T] for a in (q, k, v))
            s_all = _dot_nt(_stack_heads(q_s, hms), k_s)
            s_cat = jnp.concatenate([s_all[h * T:(h + 1) * T] * dmask_ref[h] for h in range(N_HEADS)], axis=1)
            intra.append(_dot(s_cat, _stack_heads(v_s, hms)))
    state = s_ref[...]
    for s in (reversed(range(n_sub)) if reverse else range(n_sub)):
        q_s, k_s, v_s = (a[s * T:(s + 1) * T] for a in (q, k, v))
        o = _dot(q_s * pw_q, state)
        o_ref[s * T:(s + 1) * T, :] = (o if reverse else o + intra[s]).astype(o_ref.dtype)
        state = dec_ref[...] * state + jnp.where(bd, _dot_tn(k_s * pw_k, v_s), 0.0)
    s_ref[...] = state


def _retention(cols_ret, cos_t, sin_t, dmask, pwh, pwt, dec, reverse):
    L = cols_ret.shape[0]
    rows = RET_CHUNKS * CHUNK
    nblk = L // rows
    blk = (lambda i: (nblk - 1 - i, 0)) if reverse else (lambda i: (i, 0))
    const = lambda a: pl.BlockSpec(a.shape, lambda i: (0,) * a.ndim)
    return pl.pallas_call(
        functools.partial(_ret_kernel, reverse=reverse),
        grid=(nblk,),
        in_specs=[pl.BlockSpec((rows, 4 * GROUP_W), blk),
                  pl.BlockSpec((rows, GROUP_W), blk), pl.BlockSpec((rows, GROUP_W), blk),
                  const(dmask), const(pwh), const(pwt), const(dec)],
        out_specs=pl.BlockSpec((rows, GROUP_W), blk),
        out_shape=jax.ShapeDtypeStruct((L, GROUP_W), BF16),
        scratch_shapes=[pltpu.VMEM((GROUP_W, GROUP_W), F32)],
        compiler_params=_params(("arbitrary",)),
        name="retention_bwd" if reverse else "retention_fwd",
    )(cols_ret, cos_t, sin_t, dmask, pwh, pwt, dec)


def _ssd_kernel(xbc_ref, small_ref, exp_ref, dtb_ref, a_ref, dskip_ref, o_ref, s_ref, *, reverse):
    i = pl.program_id(0)
    T = CHUNK

    @pl.when(i == 0)
    def _():
        s_ref[...] = jnp.zeros_like(s_ref)

    xbc = xbc_ref[...]
    xs = xbc[:, 0:GROUP_W]
    dt = _softplus(_dot_split(small_ref[...], exp_ref[...], 3) + dtb_ref[...])
    a = dt * a_ref[...]
    xdt = xs * dt
    _, _, vis, trif = _scan_masks(reverse)
    hms = _head_masks(T)
    prep = []
    for s in range(SCAN_CHUNKS):
        r0, r1 = s * T, (s + 1) * T
        a_cum = _dot_split_rhs(trif, a[r0:r1], 3)
        a_tot = jnp.sum(a[r0:r1], axis=0, keepdims=True)
        xdt_s = xdt[r0:r1]
        scores, c_exp, b_dec = [], [], []
        for g in range(2):
            bm = xbc[r0:r1, GROUP_W + g * SSM_STATE:GROUP_W + (g + 1) * SSM_STATE]
            cm = xbc[r0:r1, 2 * GROUP_W + g * SSM_STATE:2 * GROUP_W + (g + 1) * SSM_STATE]
            cb = _dot_nt(cm, bm)
            for h in (2 * g, 2 * g + 1):
                col = jnp.broadcast_to(a_cum[:, h * HEAD_DIM:h * HEAD_DIM + 1], (T, T))
                tot = jnp.broadcast_to(a_tot[:, h * HEAD_DIM:h * HEAD_DIM + 1], (T, T))
                scores.append(cb * jnp.where(vis, jnp.exp(col - col.T), 0.0))
                c_exp.append(cm * jnp.exp(col))
                b_dec.append(bm * jnp.exp(tot - col))
        y_in = _dot(jnp.concatenate(scores, axis=1), _stack_heads(xdt_s, hms))
        prep.append((y_in, jnp.concatenate(c_exp, axis=1).astype(BF16), jnp.concatenate(b_dec, axis=1).astype(BF16),
                     xdt_s.astype(BF16), jnp.exp(a_tot)))
    state = s_ref[...]
    for s in (reversed(range(SCAN_CHUNKS)) if reverse else range(SCAN_CHUNKS)):
        y, c_cat, b_cat, xdt_s, decay = prep[s]
        y = y + _dot(c_cat, _stack_heads(state, hms))
        upd_all = _dot_tn(b_cat, xdt_s)
        upd = jnp.zeros((SSM_STATE, GROUP_W), F32)
        for h in range(N_HEADS):
            upd = upd + jnp.where(hms[h], upd_all[h * SSM_STATE:(h + 1) * SSM_STATE], 0.0)
        state = decay * state + upd
        if not reverse:
            y = y + xs[s * T:(s + 1) * T] * dskip_ref[...]
        o_ref[s * T:(s + 1) * T, :] = y.astype(o_ref.dtype)
    s_ref[...] = state


def _scan_specs(L, reverse):
    rows = SCAN_CHUNKS * CHUNK
    nblk = L // rows
    blk = (lambda i: (nblk - 1 - i, 0)) if reverse else (lambda i: (i, 0))
    return nblk, (lambda width: pl.BlockSpec((rows, width), blk))


def _ssd(xbc_act, cols_small, expand, dt_bias, a_neg, d_skip, reverse):
    L = xbc_act.shape[0]
    nblk, rows_of = _scan_specs(L, reverse)
    const = lambda a: pl.BlockSpec(a.shape, lambda i: (0,) * a.ndim)
    return pl.pallas_call(
        functools.partial(_ssd_kernel, reverse=reverse),
        grid=(nblk,),
        in_specs=[rows_of(XBC_W), rows_of(128), const(expand), const(dt_bias), const(a_neg), const(d_skip)],
        out_specs=rows_of(GROUP_W),
        out_shape=jax.ShapeDtypeStruct((L, GROUP_W), BF16),
        scratch_shapes=[pltpu.VMEM((SSM_STATE, GROUP_W), F32)],
        compiler_params=_params(("arbitrary",)),
        name="ssd_bwd" if reverse else "ssd_fwd",
    )(xbc_act, cols_small, expand, dt_bias, a_neg, d_skip)


def _unit_tri_inverse_pairs(a_list, eye2, bd16, offs):
    bdot = functools.partial(jnp.dot, preferred_element_type=F32)
    ab = [a.astype(BF16) for a in a_list]
    n1 = [jnp.where(bd16, -a, 0.0) for a in a_list]
    t = [eye2 + x for x in n1]
    pb = [jnp.where(bd16, -a, 0.0) for a in ab]
    p = [bdot(x, _pair_diag(x)) for x in pb]
    for lvl in range(3):
        pb = [x.astype(BF16) for x in p]
        pd = [_pair_diag(x) for x in pb]
        t = [x + bdot(x.astype(BF16), d) for x, d in zip(t, pd)]
        if lvl < 2:
            p = [bdot(x, d) for x, d in zip(pb, pd)]
    for off in offs:
        tb = [x.astype(BF16) for x in t]
        m = [bdot(x, _pair_diag(jnp.where(off, a, 0.0))) for x, a in zip(tb, ab)]
        t = [x - bdot(y.astype(BF16), _pair_diag(xb)) for x, y, xb in zip(t, m, tb)]
    return t


def _gdn_kernel(qkv_ref, small_ref, expa_ref, expb_ref, dtb_ref, a_ref, o_ref, s_ref, *, reverse):
    i = pl.program_id(0)
    T = CHUNK

    @pl.when(i == 0)
    def _():
        s_ref[...] = jnp.zeros_like(s_ref)

    bdf = jnp.where(_block_diag_mask(), 1.0, 0.0)
    q = qkv_ref[:, 0:GROUP_W]
    k = qkv_ref[:, GROUP_W:2 * GROUP_W]
    v = qkv_ref[:, 2 * GROUP_W:3 * GROUP_W]
    small = small_ref[...]
    g = a_ref[...] * _softplus(_dot_split(small, expa_ref[...], 3) + dtb_ref[...])
    beta = _sigmoid(_dot_split(small, expb_ref[...], 3))
    ri, ci, vis, trif = _scan_masks(reverse)
    strict = vis & (ri != ci)
    r2 = lax.broadcasted_iota(I32, (T, 2 * T), 0)
    c2 = lax.broadcasted_iota(I32, (T, 2 * T), 1) % T
    eye2 = jnp.where(r2 == c2, 1.0, 0.0)
    bd16 = (r2 // 16) == (c2 // 16)
    offs = []
    for sz in (16, 32, 64):
        pair = (r2 // (2 * sz)) == (c2 // (2 * sz))
        later, earlier = (c2, r2) if reverse else (r2, c2)
        offs.append(pair & ((later // sz) % 2 == 1) & ((earlier // sz) % 2 == 0))
    kb = k * beta
    vb = v * beta
    hms = _head_masks(T)
    pre, a_pairs = [], []
    for s in range(SCAN_CHUNKS):
        r0, r1 = s * T, (s + 1) * T
        gc = _dot_split_rhs(trif, g[r0:r1], 3)
        g_tot = jnp.sum(g[r0:r1], axis=0, keepdims=True)
        egc = jnp.exp(gc)
        k_s, q_s = k[r0:r1], q[r0:r1]
        a_all = _dot_nt(_stack_heads(kb[r0:r1], hms), k_s)
        q_all = _dot_nt(_stack_heads(q_s, hms), k_s)
        a_h, qk_h = [], []
        for h in range(N_HEADS):
            col = jnp.broadcast_to(gc[:, h * HEAD_DIM:h * HEAD_DIM + 1], (T, T))
            decay = jnp.where(vis, jnp.exp(col - col.T), 0.0)
            a_h.append(jnp.where(strict, a_all[h * T:(h + 1) * T] * decay, 0.0))
            qk_h.append(q_all[h * T:(h + 1) * T] * decay)
        for pr in range(N_HEADS // 2):
            a_pairs.append(jnp.concatenate(a_h[2 * pr:2 * pr + 2], axis=1))
        rhs = jnp.concatenate([_stack_heads(vb[r0:r1], hms), _stack_heads(kb[r0:r1] * egc, hms)], axis=1)
        pre.append((jnp.concatenate(qk_h, axis=1).astype(BF16), rhs.astype(BF16), (q_s * egc).astype(BF16),
                    (k_s * jnp.exp(g_tot - gc)).astype(BF16), jnp.exp(g_tot)))
    t_pairs = _unit_tri_inverse_pairs(a_pairs, eye2, bd16, offs)
    npair = N_HEADS // 2
    uw = [_dot(jnp.concatenate(t_pairs[s * npair:(s + 1) * npair], axis=1), pre[s][1])
          for s in range(SCAN_CHUNKS)]
    state = s_ref[...]
    for s in (reversed(range(SCAN_CHUNKS)) if reverse else range(SCAN_CHUNKS)):
        qk_cat, _, q_dec, k_dec, decay = pre[s]
        v_new = uw[s][:, 0:GROUP_W] - _dot(uw[s][:, GROUP_W:2 * GROUP_W], state)
        o = _dot(q_dec, state) + _dot(qk_cat, _stack_heads(v_new, hms))
        o_ref[s * T:(s + 1) * T, :] = o.astype(o_ref.dtype)
        state = state * decay + bdf * _dot_tn(k_dec, v_new)
    s_ref[...] = state


def _gdn(qkv_act, cols_small, exp_a, exp_b, dt_bias, a_neg, reverse):
    L = qkv_act.shape[0]
    nblk, rows_of = _scan_specs(L, reverse)
    const = lambda a: pl.BlockSpec(a.shape, lambda i: (0,) * a.ndim)
    return pl.pallas_call(
        functools.partial(_gdn_kernel, reverse=reverse),
        grid=(nblk,),
        in_specs=[rows_of(XBC_W), rows_of(128), const(exp_a), const(exp_b), const(dt_bias), const(a_neg)],
        out_specs=rows_of(GROUP_W),
        out_shape=jax.ShapeDtypeStruct((L, GROUP_W), BF16),
        scratch_shapes=[pltpu.VMEM((GROUP_W, GROUP_W), F32)],
        compiler_params=_params(("arbitrary",)),
        name="gdn_bwd" if reverse else "gdn_fwd",
    )(qkv_act, cols_small, exp_a, exp_b, dt_bias, a_neg)


def _mixer_act_kernel(sp_ref, sm_ref, sn_ref, gp_ref, gm_ref, gn_ref, scw_ref, scb_ref, gcw_ref,
                      so_ref, go_ref, sext_ref, gext_ref):
    i = pl.program_id(0)
    n = pl.num_programs(0)
    first, last = i == 0, i == n - 1
    so_ref[...] = _silu(_conv_halo(sp_ref, sm_ref, sn_ref, scw_ref, sext_ref, first, last, SSM_CONV) + scb_ref[...])
    qkv = _silu(_conv_halo(gp_ref, gm_ref, gn_ref, gcw_ref, gext_ref, first, last, GDN_CONV))
    bdf = jnp.where(_block_diag_mask(), 1.0, 0.0)
    q = qkv[:, 0:GROUP_W]
    k = qkv[:, GROUP_W:2 * GROUP_W]
    go_ref[:, 0:GROUP_W] = q * lax.rsqrt(_dot_split(q * q, bdf, 2) + EPS) * (HEAD_DIM ** -0.5)
    go_ref[:, GROUP_W:2 * GROUP_W] = k * lax.rsqrt(_dot_split(k * k, bdf, 2) + EPS)
    go_ref[:, 2 * GROUP_W:3 * GROUP_W] = qkv[:, 2 * GROUP_W:3 * GROUP_W]


def _mixer_act(cols_ssm, cols_gdn, ssm_cw, ssm_cb, gdn_cw, tm=512):
    L = cols_ssm.shape[0]
    tm = min(tm, L)
    prev, main, nxt, _ = _halo_specs(tm, XBC_W, L // tm, False)
    const = lambda a: pl.BlockSpec(a.shape, lambda i: (0,) * a.ndim)
    return pl.pallas_call(
        _mixer_act_kernel,
        grid=(L // tm,),
        in_specs=[prev, main, nxt, prev, main, nxt, const(ssm_cw), const(ssm_cb), const(gdn_cw)],
        out_specs=[pl.BlockSpec((tm, XBC_W), lambda i: (i, 0))] * 2,
        out_shape=[jax.ShapeDtypeStruct((L, XBC_W), F32)] * 2,
        scratch_shapes=[pltpu.VMEM((tm + 2 * HALO, XBC_W), F32)] * 2,
        compiler_params=_params(("parallel",)),
        name="mixer_act",
    )(cols_ssm, cols_ssm, cols_ssm, cols_gdn, cols_gdn, cols_gdn, ssm_cw, ssm_cb, gdn_cw)


def _hy_pre_kernel(prev_ref, main_ref, next_ref, cw_ref, cb_ref, vx_ref, x0_ref, vxb_ref, ext_ref):
    i = pl.program_id(0)
    n = pl.num_programs(0)
    u = _conv_halo(prev_ref, main_ref, next_ref, cw_ref, ext_ref, i == 0, i == n - 1, HY_CONV) + cb_ref[...]
    x0_ref[...] = u[:, 0:GROUP_W]
    vx = u[:, 2 * GROUP_W:3 * GROUP_W] * u[:, GROUP_W:2 * GROUP_W]
    vx_ref[...] = vx
    vxb_ref[...] = vx.astype(BF16)


def _hy_pre(cols_hy, conv_w, conv_b, tm=512):
    L = cols_hy.shape[0]
    tm = min(tm, L)
    w = 3 * GROUP_W
    prev, main, nxt, _ = _halo_specs(tm, w, L // tm, False)
    const = lambda a: pl.BlockSpec(a.shape, lambda i: (0,) * a.ndim)
    return pl.pallas_call(
        _hy_pre_kernel,
        grid=(L // tm,),
        in_specs=[prev, main, nxt, const(conv_w), const(conv_b)],
        out_specs=[pl.BlockSpec((tm, GROUP_W), lambda i: (i, 0))] * 3,
        out_shape=[jax.ShapeDtypeStruct((L, GROUP_W), F32)] * 2 + [jax.ShapeDtypeStruct((L, GROUP_W), BF16)],
        scratch_shapes=[pltpu.VMEM((tm + 2 * HALO, w), F32)],
        compiler_params=_params(("parallel",)),
        name="hyena_pre",
    )(cols_hy, cols_hy, cols_hy, conv_w, conv_b)


def _hy_filter_kernel(z_ref, win_ref, bin_ref, wmid_ref, bmid_ref, freq_ref, wout_ref, delta_ref,
                      hf_ref, hb_ref):
    i = pl.program_id(0)
    tm = z_ref.shape[0]
    half = tm // 2
    z = z_ref[...]
    zz = jnp.concatenate([z[0:half], z[half:tm]], axis=1)
    freq = freq_ref[...]
    h = jnp.sin(freq * (_dot_hi(zz, win_ref[...]) + bin_ref[...]))
    for j in range(HY_INNER):
        h = jnp.sin(freq * (_dot_hi(h, wmid_ref[j]) + bmid_ref[j]))
    row = i * tm + lax.broadcasted_iota(I32, (half, 1), 0)
    for part in range(2):
        out = _dot_hi(h[:, part * HY_ORDER:(part + 1) * HY_ORDER], wout_ref[...])
        t = z[part * half:(part + 1) * half, 0:1]
        window = jnp.exp(-t * delta_ref[...]) + HY_SHIFT
        rows = slice(part * half, (part + 1) * half)
        hf_ref[rows, :] = (out[:, 0:GROUP_W] * window).astype(hf_ref.dtype)
        hb_ref[rows, :] = jnp.where(row + part * half == 0, 0.0,
                                    out[:, GROUP_W:2 * GROUP_W] * window).astype(hb_ref.dtype)


def _hy_filter(z_pos, w_in2, b_in2, w_mid2, b_mid2, freq2, w_out, abs_delta, tm=512):
    L = z_pos.shape[0]
    tm = min(tm, L)
    const = lambda a: pl.BlockSpec(a.shape, lambda i: (0,) * a.ndim)
    return pl.pallas_call(
        _hy_filter_kernel,
        grid=(L // tm,),
        in_specs=[pl.BlockSpec((tm, HY_EMB_PAD), lambda i: (i, 0)), const(w_in2), const(b_in2),
                  const(w_mid2), const(b_mid2), const(freq2), const(w_out), const(abs_delta)],
        out_specs=[pl.BlockSpec((tm, GROUP_W), lambda i: (i, 0))] * 2,
        out_shape=[jax.ShapeDtypeStruct((L, GROUP_W), BF16)] * 2,
        compiler_params=_params(("parallel",)),
        name="hyena_filter",
    )(z_pos, w_in2, b_in2, w_mid2, b_mid2, freq2, w_out, abs_delta)


def _left_matmul_kernel(m_ref, x_ref, o_ref):
    o_ref[...] = _dot(m_ref[...], x_ref[...]).astype(o_ref.dtype)


def _left_matmul(m, x2d, out_dtype, cb=4096):
    rows, kdim = m.shape
    ncol = x2d.shape[1]
    cb = min(cb, ncol)
    return pl.pallas_call(
        _left_matmul_kernel,
        grid=(ncol // cb,),
        in_specs=[pl.BlockSpec((rows, kdim), lambda i: (0, 0)), pl.BlockSpec((kdim, cb), lambda i: (0, i))],
        out_specs=pl.BlockSpec((rows, cb), lambda i: (0, i)),
        out_shape=jax.ShapeDtypeStruct((rows, ncol), out_dtype),
        compiler_params=_params(("parallel",)),
        name="hyena_outer_dft",
    )(m, x2d)


def _twiddle_inner_dft(a_ref, twc_ref, tws_ref, g_ref, kk):
    ar = a_ref[0, kk].astype(F32)
    ai = a_ref[1, kk].astype(F32)
    cs = twc_ref[kk]
    sn = tws_ref[kk]
    br = ar * cs + ai * sn
    bi = ai * cs - ar * sn
    x = _dot(g_ref[...], jnp.concatenate([br, bi], axis=0))
    return x[0:FFT_N2], x[FFT_N2:2 * FFT_N2]


def _hy_filter_spec_kernel(af_ref, ab_ref, twc_ref, tws_ref, g_ref, x_ref):
    for kk in range(af_ref.shape[1]):
        fr, fi = _twiddle_inner_dft(af_ref, twc_ref, tws_ref, g_ref, kk)
        br, bi = _twiddle_inner_dft(ab_ref, twc_ref, tws_ref, g_ref, kk)
        x_ref[0, kk] = (fr + br).astype(x_ref.dtype)
        x_ref[1, kk] = (fi - bi).astype(x_ref.dtype)


def _hy_conv_spec_kernel(a_ref, twc_ref, tws_ref, g_ref, gi_ref, f_ref, e_ref):
    for kk in range(a_ref.shape[1]):
        xr, xi = _twiddle_inner_dft(a_ref, twc_ref, tws_ref, g_ref, kk)
        fr = f_ref[0, kk].astype(F32)
        fi = f_ref[1, kk].astype(F32)
        yr = xr * fr - xi * fi
        yi = xr * fi + xi * fr
        dd = _dot(gi_ref[...], jnp.concatenate([yr, yi], axis=0))
        dr = dd[0:FFT_N2]
        di = dd[FFT_N2:2 * FFT_N2]
        cs = twc_ref[kk]
        sn = tws_ref[kk]
        e_ref[0, kk] = (dr * cs - di * sn).astype(e_ref.dtype)
        e_ref[1, kk] = (dr * sn + di * cs).astype(e_ref.dtype)


def _hy_spec_specs():
    blk = pl.BlockSpec((2, FFT_K1_STEP, FFT_N2, GROUP_W), lambda k: (0, k, 0, 0))
    tw = pl.BlockSpec((FFT_K1_STEP, FFT_N2, 1), lambda k: (k, 0, 0))
    mat = pl.BlockSpec((2 * FFT_N2, 2 * FFT_N2), lambda k: (0, 0))
    return blk, tw, mat


def _hy_filter_spec(af4, ab4, twc, tws, g):
    blk, tw, mat = _hy_spec_specs()
    return pl.pallas_call(
        _hy_filter_spec_kernel,
        grid=(af4.shape[1] // FFT_K1_STEP,),
        in_specs=[blk, blk, tw, tw, mat],
        out_specs=blk,
        out_shape=jax.ShapeDtypeStruct(af4.shape, BF16),
        compiler_params=_params(("parallel",)),
        name="hyena_filter_spectrum",
    )(af4, ab4, twc, tws, g)


def _hy_conv_spec(a4, twc, tws, g, gi, fspec):
    blk, tw, mat = _hy_spec_specs()
    return pl.pallas_call(
        _hy_conv_spec_kernel,
        grid=(a4.shape[1] // FFT_K1_STEP,),
        in_specs=[blk, tw, tw, mat, mat, blk],
        out_specs=blk,
        out_shape=jax.ShapeDtypeStruct(a4.shape, BF16),
        compiler_params=_params(("parallel",)),
        name="hyena_spectral_product",
    )(a4, twc, tws, g, gi, fspec)


def _postmix_kernel(h_ref, retf_ref, retb_ref, g_ref, ssdf_ref, ssdb_ref, zs_ref, hy_ref, vx_ref, x0_ref,
                    gdnf_ref, gdnb_ref, zg_ref, gnw_ref, snw_ref, hyb_ref, hnw_ref, dnw_ref,
                    wout_ref, fnw_ref, rw_ref, h1_ref, xn_ref, aff_ref):
    avg = jnp.where(_block_diag_mask(), 1.0 / HEAD_DIM, 0.0)
    o = retf_ref[...].astype(F32) + retb_ref[...].astype(F32)
    mu = _dot_split(o, avg, 2)
    var = _dot_split(jnp.square(o - mu), avg, 2)
    m_ret = _silu(g_ref[...]) * ((o - mu) * lax.rsqrt(var + EPS) * gnw_ref[...])

    y = (ssdf_ref[...].astype(F32) + ssdb_ref[...].astype(F32)) * _silu(zs_ref[...])
    m_ssm = y * lax.rsqrt(jnp.mean(y * y, axis=-1, keepdims=True) + EPS) * snw_ref[...]

    t = (hy_ref[...].astype(F32) + vx_ref[...] * hyb_ref[...]) * x0_ref[...]
    m_hy = t * lax.rsqrt(jnp.mean(t * t, axis=-1, keepdims=True) + EPS) * hnw_ref[...]

    o = gdnf_ref[...].astype(F32) + gdnb_ref[...].astype(F32)
    m_gdn = o * lax.rsqrt(_dot_split(o * o, avg, 2) + EPS) * dnw_ref[...] * _silu(zg_ref[...])

    h1 = h_ref[...]
    for gi, m in enumerate((m_ret, m_ssm, m_hy, m_gdn)):
        h1 = h1 + jnp.dot(m.astype(BF16), wout_ref[gi * GROUP_W:(gi + 1) * GROUP_W, :],
                          preferred_element_type=F32)
    h1_ref[...] = h1
    xn = h1 * lax.rsqrt(jnp.mean(h1 * h1, axis=-1, keepdims=True) + EPS) * fnw_ref[...]
    xn_ref[...] = xn.astype(BF16)
    logits = _dot_3pass(xn, rw_ref[...])
    logits = logits - jnp.max(logits, axis=-1, keepdims=True)
    ex = jnp.exp(logits)
    aff_ref[...] = ex / jnp.sum(ex, axis=-1, keepdims=True)


def _postmix(h, ret_f, ret_b, cols_ret, ssd_f, ssd_b, cols_ssm, hy_y, hy_vx, hy_x0, gdn_f, gdn_b, cols_gdn,
             gn_w, ssm_nw, hy_bias, hy_nw, gdn_nw, w_out, ffn_nw, router_w, tm=256):
    L = h.shape[0]
    tm = min(tm, L)
    row = lambda n: pl.BlockSpec((tm, n), lambda i: (i, 0))
    grp = row(GROUP_W)
    zcol = pl.BlockSpec((tm, GROUP_W), lambda i: (i, 3))
    const = lambda a: pl.BlockSpec(a.shape, lambda i: (0,) * a.ndim)
    return pl.pallas_call(
        _postmix_kernel,
        grid=(L // tm,),
        in_specs=[row(D_MODEL), grp, grp, zcol, grp, grp, zcol, grp, grp, grp, grp, grp, zcol,
                  const(gn_w), const(ssm_nw), const(hy_bias), const(hy_nw), const(gdn_nw),
                  const(w_out), const(ffn_nw), const(router_w)],
        out_specs=[row(D_MODEL), row(D_MODEL), row(N_EXPERTS)],
        out_shape=[jax.ShapeDtypeStruct((L, D_MODEL), F32), jax.ShapeDtypeStruct((L, D_MODEL), BF16),
                   jax.ShapeDtypeStruct((L, N_EXPERTS), F32)],
        compiler_params=_params(("parallel",)),
        name="postmix",
    )(h, ret_f, ret_b, cols_ret, ssd_f, ssd_b, cols_ssm, hy_y, hy_vx, hy_x0, gdn_f, gdn_b, cols_gdn,
      gn_w, ssm_nw, hy_bias, hy_nw, gdn_nw, w_out, ffn_nw, router_w)


def _topk_kernel(aff_ref, sel_ref, *, cap, idx_bits):
    aff = aff_ref[...]
    shape = aff.shape
    capf = float(cap)

    def count(mask):
        return jnp.sum(jnp.where(mask, 1.0, 0.0), axis=1, keepdims=True)

    def value_step(i, thr):
        cand = thr | lax.shift_left(jnp.int32(1), 30 - i)
        return jnp.where(count(aff >= lax.bitcast_convert_type(cand, F32)) >= capf, cand, thr)

    thr = lax.bitcast_convert_type(lax.fori_loop(0, 31, value_step, jnp.zeros((shape[0], 1), I32)), F32)
    above = aff > thr
    tied = aff == thr
    need = capf - count(above)
    idx = lax.broadcasted_iota(I32, shape, 1)

    def index_step(i, m):
        cand = m | lax.shift_left(jnp.int32(1), idx_bits - 1 - i)
        return jnp.where(count(tied & (idx < cand)) < need, cand, m)

    last = lax.fori_loop(0, idx_bits, index_step, jnp.zeros((shape[0], 1), I32))
    sel_ref[...] = jnp.where(above | (tied & (idx <= last)), 1.0, 0.0)


def _topk(aff_t, cap):
    n_e, L = aff_t.shape
    return pl.pallas_call(
        functools.partial(_topk_kernel, cap=cap, idx_bits=int(math.log2(L)) + 1),
        out_shape=jax.ShapeDtypeStruct((n_e, L), F32),
        compiler_params=_params(None),
        name="expert_topk",
    )(aff_t)


def _positions_kernel(sel_ref, posm_ref, offs_ref, *, n_e):
    s = sel_ref[...]
    nb = s.shape[0] // n_e
    r = lax.broadcasted_iota(I32, (128, 128), 0)
    c = lax.broadcasted_iota(I32, (128, 128), 1)
    incl = _dot(s, jnp.where(r <= c, 1.0, 0.0))
    tot = _dot(s, jnp.ones((128, 128), F32))
    rb = lax.broadcasted_iota(I32, (nb, nb), 0)
    cbk = lax.broadcasted_iota(I32, (nb, nb), 1)
    before = jnp.where(cbk < rb, 1.0, 0.0)
    for e in range(n_e):
        offs = _dot(before, tot[e * nb:(e + 1) * nb])
        se = s[e * nb:(e + 1) * nb]
        pos = incl[e * nb:(e + 1) * nb] - se + offs
        posm_ref[e * nb:(e + 1) * nb, :] = jnp.where(se > 0.5, pos, -1.0).astype(I32)
        offs_ref[e * nb:(e + 1) * nb, :] = offs.astype(I32)


def _positions(sel2, n_e):
    return pl.pallas_call(
        functools.partial(_positions_kernel, n_e=n_e),
        out_shape=[jax.ShapeDtypeStruct(sel2.shape, I32)] * 2,
        compiler_params=_params(None),
        name="expert_positions",
    )(sel2)


def _window_hits(posm_row, lo, cap):
    base = pl.multiple_of(jnp.minimum(lo, cap - ROW_TILE), ROW_ALIGN)
    rows = base + lax.broadcasted_iota(I32, (ROW_TILE, posm_row.shape[1]), 0)
    return base, (posm_row == rows) & (rows >= lo)


def _gather_kernel(off_ref, lo_ref, b_ref, posm_ref, aff_ref, x_ref, o_ref, og_ref):
    e = pl.program_id(0)
    nblk, _, tb = posm_ref.shape
    cap = o_ref.shape[0]
    o_ref[...] = jnp.zeros_like(o_ref)
    og_ref[...] = jnp.zeros_like(og_ref)

    def select(i):
        b = jnp.minimum(b_ref[i], nblk - 1)
        base, hit = _window_hits(posm_ref[b], lo_ref[i], cap)
        x_blk = x_ref[pl.ds(pl.multiple_of(b * tb, tb), tb), :]
        rows = jnp.dot(jnp.where(hit, 1.0, 0.0).astype(BF16), x_blk, preferred_element_type=F32).astype(BF16)
        gate = jnp.sum(jnp.where(hit, aff_ref[b], 0.0), axis=1, keepdims=True)
        return base, rows, gate

    def item_pair(ip, carry):
        picked = [select(2 * ip), select(2 * ip + 1)]
        for base, rows, gate in picked:
            o_ref[pl.ds(base, ROW_TILE), :] += rows
            og_ref[pl.ds(base, ROW_TILE), :] += gate
        return carry

    lax.fori_loop(off_ref[e] // 2, off_ref[e + 1] // 2, item_pair, 0)


def _gather(work, posm4, aff4, xn, cap):
    n_e, nblk, _, tb = posm4.shape
    per_expert = pl.BlockSpec((None, nblk, 1, tb), lambda e, *_: (e, 0, 0, 0))
    return pl.pallas_call(
        _gather_kernel,
        grid_spec=pltpu.PrefetchScalarGridSpec(
            num_scalar_prefetch=3,
            grid=(n_e,),
            in_specs=[per_expert, per_expert, pl.BlockSpec(memory_space=pltpu.VMEM)],
            out_specs=[pl.BlockSpec((None, cap, D_MODEL), lambda e, *_: (e, 0, 0)),
                       pl.BlockSpec((None, cap, 1), lambda e, *_: (e, 0, 0))],
        ),
        out_shape=[jax.ShapeDtypeStruct((n_e, cap, D_MODEL), BF16),
                   jax.ShapeDtypeStruct((n_e, cap, 1), F32)],
        compiler_params=_params(("arbitrary",), VMEM_LIMIT_RESIDENT),
        name="expert_gather",
    )(*work, posm4, aff4, xn)


def _ffn_kernel(x_ref, gate_ref, wg_ref, wu_ref, wd_ref, o_ref, acc_ref):
    f = pl.program_id(1)

    @pl.when(f == 0)
    def _():
        acc_ref[...] = jnp.zeros_like(acc_ref)

    x = x_ref[...]
    gate = jnp.dot(x, wg_ref[...].astype(BF16), preferred_element_type=F32)
    up = jnp.dot(x, wu_ref[...].astype(BF16), preferred_element_type=F32)
    acc_ref[...] += _dot(_silu(gate) * up, wd_ref[...])

    @pl.when(f == pl.num_programs(1) - 1)
    def _():
        o_ref[...] = (acc_ref[...] * gate_ref[...]).astype(BF16)


def _expert_ffn(xs, gate_c, w_gate, w_up, w_down, layer, group, fb=512):
    n_e, cap, _ = xs.shape
    n_g = n_e // EXPERT_GROUPS
    e0 = group * n_g
    return pl.pallas_call(
        _ffn_kernel,
        grid=(n_g, EXPERT_FF // fb),
        in_specs=[pl.BlockSpec((None, cap, D_MODEL), lambda e, f: (e0 + e, 0, 0)),
                  pl.BlockSpec((None, cap, 1), lambda e, f: (e0 + e, 0, 0)),
                  pl.BlockSpec((None, None, D_MODEL, fb), lambda e, f: (layer, e0 + e, 0, f)),
                  pl.BlockSpec((None, None, D_MODEL, fb), lambda e, f: (layer, e0 + e, 0, f)),
                  pl.BlockSpec((None, None, fb, D_MODEL), lambda e, f: (layer, e0 + e, f, 0))],
        out_specs=pl.BlockSpec((None, cap, D_MODEL), lambda e, f: (e, 0, 0)),
        out_shape=jax.ShapeDtypeStruct((n_g, cap, D_MODEL), BF16),
        scratch_shapes=[pltpu.VMEM((cap, D_MODEL), F32)],
        compiler_params=_params(("parallel", "arbitrary"), VMEM_LIMIT_RESIDENT),
        name="expert_ffn",
    )(xs, gate_c, w_gate, w_up, w_down)


def _scatter_kernel(off_ref, lo_ref, e_ref, posm_ref, y_ref, o_ref, acc_ref):
    b = pl.program_id(0)
    n_g, cap, _ = y_ref.shape
    acc_ref[...] = jnp.zeros_like(acc_ref)

    def pick(i):
        e = jnp.minimum(e_ref[i], n_g - 1)
        base, hit = _window_hits(posm_ref[e], lo_ref[i], cap)
        return jnp.where(hit, 1.0, 0.0).astype(BF16), y_ref[e, pl.ds(base, ROW_TILE), :]

    def item_pair(ip, carry):
        p0, y0 = pick(2 * ip)
        p1, y1 = pick(2 * ip + 1)
        acc_ref[...] += lax.dot_general(jnp.concatenate([p0, p1], axis=0), jnp.concatenate([y0, y1], axis=0),
                                        (((0,), (0,)), ((), ())), preferred_element_type=F32)
        return carry

    lax.fori_loop(off_ref[b] // 2, off_ref[b + 1] // 2, item_pair, 0)
    o_ref[...] = acc_ref[...].astype(o_ref.dtype)


def _scatter(work, posm4, ye, group):
    n_e, nblk, _, tb = posm4.shape
    n_g = ye.shape[0]
    return pl.pallas_call(
        _scatter_kernel,
        grid_spec=pltpu.PrefetchScalarGridSpec(
            num_scalar_prefetch=3,
            grid=(nblk,),
            in_specs=[pl.BlockSpec((n_g, None, 1, tb), lambda b, *_: (group, b, 0, 0)),
                      pl.BlockSpec(memory_space=pltpu.VMEM)],
            out_specs=pl.BlockSpec((tb, D_MODEL), lambda b, *_: (b, 0)),
            scratch_shapes=[pltpu.VMEM((tb, D_MODEL), F32)],
        ),
        out_shape=jax.ShapeDtypeStruct((nblk * tb, D_MODEL), BF16),
        compiler_params=_params(("arbitrary",), VMEM_LIMIT_RESIDENT),
        name="expert_scatter",
    )(*work, posm4, ye)


def _enumerate_items(n_items, first_row, cap, n_work):
    n_groups, n_members = n_items.shape
    parity = jnp.sum(n_items, axis=1, keepdims=True) % 2
    flat = jnp.concatenate([n_items, parity], axis=1).reshape(-1)
    first = jnp.concatenate([first_row, jnp.full((n_groups, 1), cap, I32)], axis=1).reshape(-1)
    ends = jnp.cumsum(flat)
    begins = ends - flat
    w = jnp.arange(n_work, dtype=I32)
    pair = jnp.sum((ends[None, :] <= w[:, None]).astype(I32), axis=1)
    onehot = pair[:, None] == jnp.arange(flat.shape[0], dtype=I32)[None, :]
    begin_w = jnp.sum(jnp.where(onehot, begins[None, :], 0), axis=1)
    first_w = jnp.sum(jnp.where(onehot, first[None, :], 0), axis=1)
    offsets = jnp.concatenate([jnp.zeros((1,), I32), ends.reshape(n_groups, n_members + 1)[:, -1]])
    lo = first_w + ROW_TILE * (w - begin_w)
    return offsets.astype(I32), lo.astype(I32), (pair % (n_members + 1)).astype(I32)


def _work_lists(offs, cap, tb):
    n_e, nb128 = offs.shape
    per = tb // 128
    nblk = nb128 // per
    start = offs[:, ::per]
    end = jnp.concatenate([start[:, 1:], jnp.full((n_e, 1), cap, I32)], axis=1)
    first_row = (start // ROW_ALIGN) * ROW_ALIGN
    n_items = jnp.where(end > start, (end - first_row + ROW_TILE - 1) // ROW_TILE, 0)
    per_expert = (cap + (ROW_TILE + ROW_ALIGN - 2) * nblk) // ROW_TILE + 2
    off_g, lo_g, b_g = _enumerate_items(n_items, first_row, cap, n_e * per_expert)
    n_g = n_e // EXPERT_GROUPS
    scatter_work = []
    for grp in range(EXPERT_GROUPS):
        sl = slice(grp * n_g, (grp + 1) * n_g)
        scatter_work.append(_enumerate_items(n_items[sl].T, first_row[sl].T, cap, n_g * per_expert + nblk))
    return (off_g, lo_g, b_g), scatter_work


def _ple_kernel(h_ref, moe_a_ref, moe_b_ref, p_ref, nw_ref, wg_ref, wp_ref, fw_ref, o_ref, *, final):
    h2 = h_ref[...] + (moe_a_ref[...].astype(F32) + moe_b_ref[...].astype(F32))
    hn = h2 * lax.rsqrt(jnp.mean(h2 * h2, axis=-1, keepdims=True) + EPS) * nw_ref[...]
    gate = _sigmoid(jnp.dot(hn.astype(BF16), wg_ref[...], preferred_element_type=F32))
    h3 = h2 + gate * jnp.dot(p_ref[...].astype(BF16), wp_ref[...], preferred_element_type=F32)
    if final:
        h3 = h3 * lax.rsqrt(jnp.mean(h3 * h3, axis=-1, keepdims=True) + EPS) * fw_ref[...]
    o_ref[...] = h3


def _ple(h1, moe_a, moe_b, p, layer, norm_w, w_gate, w_proj, final_w, final, tm=512):
    L = h1.shape[0]
    tm = min(tm, L)
    row = lambda n: pl.BlockSpec((tm, n), lambda i: (i, 0))
    const = lambda a: pl.BlockSpec(a.shape, lambda i: (0,) * a.ndim)
    return pl.pallas_call(
        functools.partial(_ple_kernel, final=final),
        grid=(L // tm,),
        in_specs=[row(D_MODEL), row(D_MODEL), row(D_MODEL),
                  pl.BlockSpec((None, None, tm, PLE_DIM), lambda i: (layer, 0, i, 0)),
                  const(norm_w), const(w_gate), const(w_proj), const(final_w)],
        out_specs=row(D_MODEL),
        out_shape=jax.ShapeDtypeStruct((L, D_MODEL), F32),
        compiler_params=_params(("parallel",)),
        name="ple",
    )(h1, moe_a, moe_b, p, norm_w, w_gate, w_proj, final_w)


def _expand_table(base):
    t = np.zeros((2, 128, GROUP_W), np.float32)
    for d in range(2):
        for h in range(N_HEADS):
            t[d, base + N_HEADS * d + h, h * HEAD_DIM:(h + 1) * HEAD_DIM] = 1.0
    return t


def _retention_tables():
    hh = np.arange(N_HEADS, dtype=np.float64)
    lg = np.log(1.0 - 2.0 ** (-5.0 - hh))
    t = np.arange(CHUNK, dtype=np.float64)
    dmask = np.exp(np.abs(t[:, None] - t[None, :])[None] * lg[:, None, None])
    lanes = np.repeat(lg, HEAD_DIM)[None, :]
    pwt = np.exp((CHUNK - t)[:, None] * lanes)
    pwh = np.exp(t[:, None] * lanes)
    dec = np.exp(CHUNK * lanes)
    return [a.astype(np.float32) for a in (dmask, pwh, pwt, dec)]


def _dft_tables(n1):
    n = n1 * FFT_N2
    nh = n1 // 2 + 1
    nhp = -(-nh // FFT_K1_STEP) * FFT_K1_STEP
    k1 = np.arange(nh, dtype=np.float64)
    t1 = np.arange(n1, dtype=np.float64)
    ang1 = 2.0 * np.pi * np.outer(k1, t1) / n1
    f1 = np.zeros((2 * nhp, n1))
    f1[0:nh] = np.cos(ang1)
    f1[nhp:nhp + nh] = -np.sin(ang1)
    weight = np.where((k1 == 0) | (k1 == n1 // 2), 1.0, 2.0)[None, :] / n
    f3 = np.zeros((n1 // 2, 2 * nhp))
    f3[:, 0:nh] = np.cos(ang1.T[:n1 // 2]) * weight
    f3[:, nhp:nhp + nh] = -np.sin(ang1.T[:n1 // 2]) * weight
    k2 = np.arange(FFT_N2, dtype=np.float64)
    ang2 = 2.0 * np.pi * np.outer(k2, k2) / FFT_N2
    c2, s2 = np.cos(ang2), np.sin(ang2)
    g = np.block([[c2, s2], [-s2, c2]])
    gi = np.block([[c2, -s2], [s2, c2]])
    angt = np.zeros((nhp, FFT_N2))
    angt[0:nh] = 2.0 * np.pi * np.outer(k1, k2) / n
    twc, tws = np.cos(angt)[:, :, None], np.sin(angt)[:, :, None]
    return [a.astype(np.float32) for a in (f1, f3, g, gi, twc, tws)]


def _hyena_positions(L):
    m = jnp.arange(L, dtype=F32)[:, None]
    t = m / (L - 1)
    bands = jnp.linspace(1e-4, HY_BANDS - 1, HY_BANDS, dtype=F32)
    ang = (2.0 * math.pi / L) * m * bands[None]
    return jnp.concatenate([t, jnp.cos(ang), -jnp.sin(ang), jnp.zeros((L, HY_EMB_PAD - HY_EMB), F32)], axis=-1)


def _rope_tables(L):
    inv = ROPE_BASE ** (-jnp.arange(0, HEAD_DIM, 2, dtype=F32) / HEAD_DIM)
    ang = jnp.arange(L, dtype=F32)[:, None] * inv[None]
    cos, sin = jnp.cos(ang), jnp.sin(ang)
    cos_t = jnp.tile(jnp.concatenate([cos, cos], axis=-1), (1, N_HEADS))
    sin_t = jnp.tile(jnp.concatenate([-sin, sin], axis=-1), (1, N_HEADS))
    return cos_t, sin_t


def _lanes(v):
    return jnp.repeat(v, HEAD_DIM, axis=-1)[..., None, :]


def _twice_diag(w):
    z = jnp.zeros_like(w)
    return jnp.concatenate([jnp.concatenate([w, z], axis=-1), jnp.concatenate([z, w], axis=-1)], axis=-2)


def kernel(x, p, norm_mix_w, w_in, ret_gn_w, ssm_conv_w, ssm_conv_b, ssm_a_log, ssm_dt_bias, ssm_d, ssm_norm_w, hy_conv_w, hy_conv_b, hy_filt_w_in, hy_filt_b_in, hy_filt_w_mid, hy_filt_b_mid, hy_filt_freq, hy_filt_w_out, hy_bias, hy_norm_w, gdn_conv_w, gdn_a_log, gdn_dt_bias, gdn_norm_w, w_out, norm_ffn_w, router_w, exp_w_gate, exp_w_up, exp_w_down, ple_norm_w, ple_gate_w, ple_proj_w, final_norm_w):
    batch, L, _ = x.shape
    depth = w_in.shape[0]
    assert batch == 1 and L % TOKEN_BLOCK == 0 and (2 * L) % (8 * FFT_N2) == 0
    cap = CAPACITY_FACTOR * L // N_EXPERTS
    assert cap % ROW_TILE == 0
    n1 = 2 * L // FFT_N2
    nblk = L // TOKEN_BLOCK

    dmask, pwh, pwt, dec = (jnp.asarray(a) for a in _retention_tables())
    f1, f3, g_mat, gi_mat, twc, tws = (jnp.asarray(a) for a in _dft_tables(n1))
    exp_dt, exp_a, exp_b = (jnp.asarray(_expand_table(b)) for b in (0, 8, 16))
    cos_t, sin_t = _rope_tables(L)
    z_pos = _hyena_positions(L)
    f1_half = f1[:, :n1 // 2]
    max_decay = math.log(HY_DECAY_TARGET) / HY_FAST_PCT
    min_decay = math.log(HY_DECAY_TARGET) / HY_SLOW_PCT
    abs_delta = jnp.abs(jnp.linspace(min_decay, max_decay, GROUP_W, dtype=F32))[None, :]
    row = lambda v: v[None, :]
    twice = lambda v: jnp.concatenate([v, v], axis=-1)

    h = x[0]
    for i in range(depth):
        w = w_in[i]
        o_ssm = 4 * GROUP_W
        o_hy = o_ssm + GROUP_W + XBC_W + 2 * N_HEADS
        o_gdn = o_hy + 3 * GROUP_W
        o_ab = o_gdn + 4 * GROUP_W
        w_ret = w[:, 0:o_ssm].astype(BF16)
        w_ssm = jnp.concatenate([w[:, o_ssm + GROUP_W:o_ssm + GROUP_W + XBC_W], w[:, o_ssm:o_ssm + GROUP_W]],
                                axis=1).astype(BF16)
        w_hy = w[:, o_hy:o_gdn].astype(BF16)
        w_gdn = w[:, o_gdn:o_ab].astype(BF16)
        w_small = jnp.concatenate([w[:, o_hy - 2 * N_HEADS:o_hy], w[:, o_ab:o_ab + 4 * N_HEADS],
                                   jnp.zeros((D_MODEL, 128 - 6 * N_HEADS), F32)], axis=1).astype(BF16)
        c_ret, c_ssm, c_hy, c_gdn, c_small = _inproj(h, row(norm_mix_w[i]), w_ret, w_ssm, w_hy, w_gdn, w_small)

        ret_f, ret_b = (_retention(c_ret, cos_t, sin_t, dmask, pwh, pwt, dec, reverse=bool(d)) for d in range(2))

        ssm_dtb = _lanes(ssm_dt_bias[i])
        ssm_a = _lanes(-jnp.exp(ssm_a_log[i]))
        xbc_act, qkv_act = _mixer_act(c_ssm, c_gdn, ssm_conv_w[i], row(ssm_conv_b[i]), gdn_conv_w[i])
        ssd_f, ssd_b = (_ssd(xbc_act, c_small, exp_dt[d], ssm_dtb[d], ssm_a[d], _lanes(ssm_d[i]), reverse=bool(d))
                        for d in range(2))

        gdn_dtb = _lanes(gdn_dt_bias[i])
        gdn_a = _lanes(-jnp.exp(gdn_a_log[i]))
        gdn_f, gdn_b = (_gdn(qkv_act, c_small, exp_a[d], exp_b[d], gdn_dtb[d], gdn_a[d], reverse=bool(d))
                        for d in range(2))

        hy_vx, hy_x0, hy_vxb = _hy_pre(c_hy, hy_conv_w[i], row(hy_conv_b[i]))
        w_in_pad = jnp.concatenate([hy_filt_w_in[i], jnp.zeros((HY_EMB_PAD - HY_EMB, HY_ORDER), F32)], axis=0)
        taps = _hy_filter(z_pos, _twice_diag(w_in_pad), row(twice(hy_filt_b_in[i])), _twice_diag(hy_filt_w_mid[i]),
                          twice(hy_filt_b_mid[i])[:, None, :], row(twice(hy_filt_freq[i])), hy_filt_w_out[i],
                          abs_delta)
        nk1 = f1.shape[0] // 2
        spec_shape = (2, nk1, FFT_N2, GROUP_W)
        outer_dft = lambda sig: _left_matmul(f1_half, sig.reshape(n1 // 2, FFT_N2 * GROUP_W), BF16).reshape(spec_shape)
        f_spec = _hy_filter_spec(outer_dft(taps[0]), outer_dft(taps[1]), twc, tws, g_mat)
        e_spec = _hy_conv_spec(outer_dft(hy_vxb), twc, tws, g_mat, gi_mat, f_spec)
        hy_y = _left_matmul(f3, e_spec.reshape(2 * nk1, FFT_N2 * GROUP_W), BF16).reshape(L, GROUP_W)

        h1, xn, aff = _postmix(h, ret_f, ret_b, c_ret, ssd_f, ssd_b, c_ssm, hy_y, hy_vx, hy_x0, gdn_f, gdn_b, c_gdn,
                               row(ret_gn_w[i]), row(ssm_norm_w[i]), row(hy_bias[i]), row(hy_norm_w[i]),
                               row(jnp.tile(gdn_norm_w[i], N_HEADS)), w_out[i].astype(BF16),
                               row(norm_ffn_w[i]), router_w[i])

        aff_t = aff.T
        sel = _topk(aff_t, cap)
        posm, offs = _positions(sel.reshape(N_EXPERTS * (L // 128), 128), N_EXPERTS)
        posm4 = posm.reshape(N_EXPERTS, nblk, 1, TOKEN_BLOCK)
        aff4 = aff_t.reshape(N_EXPERTS, nblk, 1, TOKEN_BLOCK)
        gather_work, scatter_work = _work_lists(offs[:, 0].reshape(N_EXPERTS, L // 128), cap, TOKEN_BLOCK)
        xs, gate_c = _gather(gather_work, posm4, aff4, xn, cap)
        moe = [_scatter(scatter_work[grp], posm4,
                        _expert_ffn(xs, gate_c, exp_w_gate, exp_w_up, exp_w_down, i, grp), grp)
               for grp in range(EXPERT_GROUPS)]

        h = _ple(h1, moe[0], moe[1], p, i, row(ple_norm_w[i]), ple_gate_w[i].astype(BF16), ple_proj_w[i].astype(BF16),
                 row(final_norm_w), final=(i == depth - 1))
    return h[None]
```
